```python
import jax
import jax.numpy as jnp
from jax import lax
import numpy as np

D_MODEL = 1024
BATCH = 4
SEQ = 4096
DEPTH = 4
DEC_BATCH = 8
DEC_SEQ = 32
PAST_LEN = 1024

CHUNK = 64
HEAD_DIM = 64
ROPE_THETA = 10000.0
EPS = 1e-6
A_HEADS = 8
A_KV_HEADS = 2
WINDOW = 128
A_PREV_CHUNKS = WINDOW // CHUNK
B_HEADS = 8
B_PREV_CHUNKS = 8
B_REACH = B_PREV_CHUNKS * CHUNK
REL_CLIP = 128
C_HEADS = 4
C_DK = 64
C_DV = 128
D_HEADS = 4
D_DK = 64
D_DV = 128
GLA_RANK = 16
GLA_TAU = 16.0
N_BRANCHES = 4
BRANCH_WIDTH = 512
MEM_LEN = 256
X_HEADS = 4
X_HEAD_DIM = 128
D_FF = 4 * D_MODEL

IN_WIDTHS = (A_HEADS * HEAD_DIM, A_KV_HEADS * HEAD_DIM, A_KV_HEADS * HEAD_DIM,
             B_HEADS * HEAD_DIM, B_HEADS * HEAD_DIM, B_HEADS * HEAD_DIM,
             C_HEADS * C_DK, C_HEADS * C_DK, C_HEADS * C_DV, C_HEADS * C_DV,
             D_HEADS * D_DK, D_HEADS * D_DK, D_HEADS * D_DV, D_HEADS * D_DV, GLA_RANK)
IN_WIDTH = sum(IN_WIDTHS)

kernel_name = 'hybrid_streaming_encoder_step'


def _rms(x, g):
    xf = x.astype(jnp.float32)
    y = xf * lax.rsqrt(jnp.mean(xf * xf, axis=-1, keepdims=True) + EPS)
    return (y * g.astype(jnp.float32)).astype(x.dtype)


def _group_ln(x, g):
    xf = x.astype(jnp.float32)
    xc = xf - jnp.mean(xf, axis=-1, keepdims=True)
    y = xc * lax.rsqrt(jnp.mean(xc * xc, axis=-1, keepdims=True) + EPS)
    return (y * g.astype(jnp.float32)).astype(x.dtype)


def _rope(x, pos):
    half = x.shape[-1] // 2
    freqs = ROPE_THETA ** (-jnp.arange(half, dtype=jnp.float32) / half)
    ang = pos.astype(jnp.float32)[:, None] * freqs[None, :]
    cos = jnp.cos(ang)[None, :, None, :]
    sin = jnp.sin(ang)[None, :, None, :]
    xf = x.astype(jnp.float32)
    x1, x2 = xf[..., :half], xf[..., half:]
    return jnp.concatenate([x1 * cos - x2 * sin, x2 * cos + x1 * sin], axis=-1).astype(x.dtype)


def _band_attend(q, k, v, q_pos, k_pos, sink, bias_table):
    B, Tq, H, Dh = q.shape
    Tk, Hkv = k.shape[1], k.shape[2]
    G = H // Hkv
    qg = q.reshape(B, Tq, Hkv, G, Dh)
    s = jnp.einsum('bqhgd,bkhd->bhgqk', qg, k).astype(jnp.float32)
    if bias_table is not None:
        rel = jnp.clip(q_pos[:, None] - k_pos[None, :], -REL_CLIP, REL_CLIP) + REL_CLIP
        s = s + bias_table[:, rel].astype(jnp.float32).reshape(Hkv, G, Tq, Tk)
    s = jnp.where(k_pos >= 0, s, -jnp.inf)
    if sink is not None:
        sink_b = sink.astype(jnp.float32).reshape(1, Hkv, G, 1, 1)
        lse = jnp.logaddexp(jax.nn.logsumexp(s, axis=-1, keepdims=True), sink_b)
        p = jnp.exp(s - lse)
    else:
        p = jax.nn.softmax(s, axis=-1)
    o = jnp.einsum('bhgqk,bkhd->bqhgd', p.astype(v.dtype), v)
    return o.reshape(B, Tq, H * Dh)


def _band_prompt(q, k, v, n_prev, sink, bias_table):
    B, T, H, Dh = q.shape
    n_chunks = T // CHUNK
    pad = n_prev * CHUNK
    band = pad + CHUNK
    kp = jnp.pad(k, ((0, 0), (pad, 0), (0, 0), (0, 0)))
    vp = jnp.pad(v, ((0, 0), (pad, 0), (0, 0), (0, 0)))
    qc = jnp.moveaxis(q.reshape(B, n_chunks, CHUNK, H, Dh), 1, 0)

    def one_chunk(args):
        c, qb = args
        start = c * CHUNK
        kb = lax.dynamic_slice_in_dim(kp, start, band, axis=1)
        vb = lax.dynamic_slice_in_dim(vp, start, band, axis=1)
        q_pos = start + jnp.arange(CHUNK)
        k_pos = start - pad + jnp.arange(band)
        return _band_attend(qb, kb, vb, q_pos, k_pos, sink, bias_table)

    out = lax.map(one_chunk, (jnp.arange(n_chunks), qc))
    return jnp.moveaxis(out, 0, 1).reshape(B, T, H * Dh)


def _band_step(q, k_new, v_new, k_cache, v_cache, q_pos, sink, bias_table):
    L = k_cache.shape[1]
    k = jnp.concatenate([k_cache, k_new], axis=1)
    v = jnp.concatenate([v_cache, v_new], axis=1)
    k_pos = jnp.concatenate([PAST_LEN - L + jnp.arange(L), q_pos])
    return _band_attend(q, k, v, q_pos, k_pos, sink, bias_table)


def _decay_scan(q, k, v, log_a, S0, chunk_len):
    B, T, H, Dk = q.shape
    Dv = v.shape[-1]
    n = T // chunk_len

    def chunks(t):
        return jnp.moveaxis(t.reshape(B, n, chunk_len, H, t.shape[-1]), 1, 0)

    causal = jnp.tril(jnp.ones((chunk_len, chunk_len), dtype=bool))[None, :, :, None, None]

    def step(S, inp):
        qc, kc, vc, ac = inp
        qf, kf, vf = qc.astype(jnp.float32), kc.astype(jnp.float32), vc.astype(jnp.float32)
        Sf = S.astype(jnp.float32)
        cum = jnp.cumsum(ac.astype(jnp.float32), axis=1)
        rel = jnp.where(causal, cum[:, :, None] - cum[:, None, :], -jnp.inf)
        scores = jnp.einsum('bihd,bjhd,bijhd->bhij', qf, kf, jnp.exp(rel))
        o = jnp.einsum('bhij,bjhv->bihv', scores, vf)
        o = o + jnp.einsum('bihd,bhdv->bihv', qf * jnp.exp(cum), Sf)
        last = cum[:, -1]
        S_new = Sf * jnp.exp(last)[..., None] + jnp.einsum(
            'bjhd,bjhv->bhdv', kf * jnp.exp(last[:, None] - cum), vf)
        return S_new.astype(S.dtype), o.astype(v.dtype)

    S_fin, o = lax.scan(step, S0, (chunks(q), chunks(k), chunks(v), chunks(log_a)))
    return jnp.moveaxis(o, 0, 1).reshape(B, T, H, Dv), S_fin


def _merge(h, outs, l, P):
    B, T, _ = h.shape
    gates = jax.nn.sigmoid((h @ P['w_merge'][l]).astype(jnp.float32)).astype(h.dtype)
    gates = gates.reshape(B, T, N_BRANCHES, D_MODEL)
    br = jnp.einsum('btnc,ncd->btnd', jnp.stack(outs, axis=-2), P['w_branch'][l])
    return jnp.sum(gates * br, axis=-2) @ P['w_out'][l]


def _mem_kv(mem, l, P):
    B, M, _ = mem.shape
    kv = _rms(mem, P['norm_mem'][l]) @ P['w_xkv'][l]
    k, v = jnp.split(kv, 2, axis=-1)
    k = _rms(k.reshape(B, M, X_HEADS, X_HEAD_DIM), P['kn_x'][l])
    return k, v.reshape(B, M, X_HEADS, X_HEAD_DIM)


def _cross(h, mk, mv, l, P):
    B, T, _ = h.shape
    q = _rms((h @ P['w_xq'][l]).reshape(B, T, X_HEADS, X_HEAD_DIM), P['qn_x'][l]) * X_HEAD_DIM ** -0.5
    s = jnp.einsum('bqhd,bkhd->bhqk', q, mk).astype(jnp.float32)
    p = jax.nn.softmax(s, axis=-1).astype(mv.dtype)
    o = jnp.einsum('bhqk,bkhd->bqhd', p, mv).reshape(B, T, X_HEADS * X_HEAD_DIM)
    return o @ P['w_xo'][l]


def _layer(x, pos, l, P, mem_k, mem_v, cache):
    B, T, _ = x.shape
    h = _rms(x, P['norm_mix'][l])
    z = h @ P['w_in'][l]
    (qa, ka, va, qb, kb, vb, qc, kc, vc, gc, qd, kd, vd, gd, ad) = jnp.split(
        z, np.cumsum(IN_WIDTHS)[:-1].tolist(), axis=-1)
    sc = HEAD_DIM ** -0.5
    qa = _rope(_rms(qa.reshape(B, T, A_HEADS, HEAD_DIM), P['qn_a'][l]), pos) * sc
    ka = _rope(_rms(ka.reshape(B, T, A_KV_HEADS, HEAD_DIM), P['kn_a'][l]), pos)
    va = va.reshape(B, T, A_KV_HEADS, HEAD_DIM)
    qb = _rms(qb.reshape(B, T, B_HEADS, HEAD_DIM), P['qn_b'][l]) * sc
    kb = _rms(kb.reshape(B, T, B_HEADS, HEAD_DIM), P['kn_b'][l])
    vb = vb.reshape(B, T, B_HEADS, HEAD_DIM)
    qc = _rope(qc.reshape(B, T, C_HEADS, C_DK), pos)
    kc = _rope(kc.reshape(B, T, C_HEADS, C_DK), pos) * C_DK ** -0.5
    vc = vc.reshape(B, T, C_HEADS, C_DV)
    log_gamma = jnp.log1p(-jnp.exp2(-5.0 - jnp.arange(C_HEADS, dtype=jnp.float32)))
    la_c = jnp.broadcast_to(log_gamma[None, None, :, None], (B, T, C_HEADS, C_DK))
    qd = qd.reshape(B, T, D_HEADS, D_DK) * D_DK ** -0.5
    kd = kd.reshape(B, T, D_HEADS, D_DK)
    vd = vd.reshape(B, T, D_HEADS, D_DV)
    la_d = (jax.nn.log_sigmoid((ad @ P['w_alpha_up'][l] + P['b_alpha'][l]).astype(jnp.float32))
            / GLA_TAU).reshape(B, T, D_HEADS, D_DK)
    if cache is None:
        oa = _band_prompt(qa, ka, va, A_PREV_CHUNKS, P['sink_a'][l], None)
        ob = _band_prompt(qb, kb, vb, B_PREV_CHUNKS, None, P['relbias_b'][l])
        s_c0 = jnp.zeros((B, C_HEADS, C_DK, C_DV), x.dtype)
        s_d0 = jnp.zeros((B, D_HEADS, D_DK, D_DV), x.dtype)
        block = CHUNK
        ra, rb = min(WINDOW, T), min(B_REACH, T)
        new_rows = (ka[:, T - ra:], va[:, T - ra:], kb[:, T - rb:], vb[:, T - rb:])
    else:
        win_k, win_v, band_k, band_v, s_c0, s_d0 = cache
        oa = _band_step(qa, ka, va, win_k, win_v, pos, P['sink_a'][l], None)
        ob = _band_step(qb, kb, vb, band_k, band_v, pos, None, P['relbias_b'][l])
        block = T
        new_rows = (ka, va, kb, vb)
    oc, s_c = _decay_scan(qc, kc, vc, la_c, s_c0, block)
    oc = _group_ln(oc, P['gn_c'][l].reshape(C_HEADS, C_DV)).reshape(B, T, C_HEADS * C_DV) * jax.nn.silu(gc)
    od, s_d = _decay_scan(qd, kd, vd, la_d, s_d0, block)
    od = _rms(od, P['gn_d'][l].reshape(D_HEADS, D_DV)).reshape(B, T, D_HEADS * D_DV) * jax.nn.silu(gd)
    x = x + _merge(h, (oa, ob, oc, od), l, P)
    x = x + _cross(_rms(x, P['norm_x'][l]), mem_k, mem_v, l, P)
    hf = _rms(x, P['norm_ffn'][l])
    x = x + jnp.square(jax.nn.relu(hf @ P['w_up'][l])) @ P['w_down'][l]
    return x, new_rows + (s_c, s_d)


def setup_inputs(seed: int = 0) -> dict:
    key = jax.random.key(seed)
    keys = iter(jax.random.split(key, 64))

    def nrm(shape, scale):
        return jax.random.normal(next(keys), shape, jnp.float32) * scale

    def gain(shape):
        return 1.0 + nrm(shape, 0.05)

    la = min(WINDOW, PAST_LEN)
    lb = min(B_REACH, PAST_LEN)
    xw = X_HEADS * X_HEAD_DIM
    return {
        'x_prompt': nrm((BATCH, SEQ, D_MODEL), 1.0),
        'x_sample': nrm((DEC_BATCH, DEC_SEQ, D_MODEL), 1.0),
        'cache_win_k': nrm((DEPTH, DEC_BATCH, la, A_KV_HEADS, HEAD_DIM), 1.0),
        'cache_win_v': nrm((DEPTH, DEC_BATCH, la, A_KV_HEADS, HEAD_DIM), 1.0),
        'cache_band_k': nrm((DEPTH, DEC_BATCH, lb, B_HEADS, HEAD_DIM), 1.0),
        'cache_band_v': nrm((DEPTH, DEC_BATCH, lb, B_HEADS, HEAD_DIM), 1.0),
        'state_ret': nrm((DEPTH, DEC_BATCH, C_HEADS, C_DK, C_DV), 0.1),
        'state_gla': nrm((DEPTH, DEC_BATCH, D_HEADS, D_DK, D_DV), 0.1),
        'cache_mem_k': nrm((DEPTH, DEC_BATCH, MEM_LEN, X_HEADS, X_HEAD_DIM), 1.0),
        'cache_mem_v': nrm((DEPTH, DEC_BATCH, MEM_LEN, X_HEADS, X_HEAD_DIM), 1.0),
        'mem_prompt': nrm((BATCH, MEM_LEN, D_MODEL), 1.0),
        'norm_mix': gain((DEPTH, D_MODEL)),
        'w_in': nrm((DEPTH, D_MODEL, IN_WIDTH), D_MODEL ** -0.5),
        'qn_a': gain((DEPTH, HEAD_DIM)),
        'kn_a': gain((DEPTH, HEAD_DIM)),
        'sink_a': nrm((DEPTH, A_HEADS), 0.5),
        'qn_b': gain((DEPTH, HEAD_DIM)),
        'kn_b': gain((DEPTH, HEAD_DIM)),
        'relbias_b': nrm((DEPTH, B_HEADS, 2 * REL_CLIP + 1), 0.5),
        'gn_c': gain((DEPTH, C_HEADS * C_DV)),
        'w_alpha_up': nrm((DEPTH, GLA_RANK, D_HEADS * D_DK), GLA_RANK ** -0.5),
        'b_alpha': nrm((DEPTH, D_HEADS * D_DK), 0.5),
        'gn_d': gain((DEPTH, D_HEADS * D_DV)),
        'w_branch': nrm((DEPTH, N_BRANCHES, BRANCH_WIDTH, D_MODEL), BRANCH_WIDTH ** -0.5),
        'w_merge': nrm((DEPTH, D_MODEL, N_BRANCHES * D_MODEL), D_MODEL ** -0.5),
        'w_out': nrm((DEPTH, D_MODEL, D_MODEL), D_MODEL ** -0.5),
        'norm_x': gain((DEPTH, D_MODEL)),
        'norm_mem': gain((DEPTH, D_MODEL)),
        'w_xq': nrm((DEPTH, D_MODEL, xw), D_MODEL ** -0.5),
        'w_xkv': nrm((DEPTH, D_MODEL, 2 * xw), D_MODEL ** -0.5),
        'qn_x': gain((DEPTH, X_HEAD_DIM)),
        'kn_x': gain((DEPTH, X_HEAD_DIM)),
        'w_xo': nrm((DEPTH, xw, D_MODEL), xw ** -0.5),
        'norm_ffn': gain((DEPTH, D_MODEL)),
        'w_up': nrm((DEPTH, D_MODEL, D_FF), D_MODEL ** -0.5),
        'w_down': nrm((DEPTH, D_FF, D_MODEL), D_FF ** -0.5),
    }


def reference(x_prompt, x_sample, cache_win_k, cache_win_v, cache_band_k, cache_band_v,
              state_ret, state_gla, cache_mem_k, cache_mem_v, mem_prompt,
              norm_mix, w_in, qn_a, kn_a, sink_a, qn_b, kn_b, relbias_b, gn_c,
              w_alpha_up, b_alpha, gn_d, w_branch, w_merge, w_out, norm_x, norm_mem,
              w_xq, w_xkv, qn_x, kn_x, w_xo, norm_ffn, w_up, w_down):
    P = dict(norm_mix=norm_mix, w_in=w_in, qn_a=qn_a, kn_a=kn_a, sink_a=sink_a, qn_b=qn_b,
             kn_b=kn_b, relbias_b=relbias_b, gn_c=gn_c, w_alpha_up=w_alpha_up, b_alpha=b_alpha,
             gn_d=gn_d, w_branch=w_branch, w_merge=w_merge, w_out=w_out, norm_x=norm_x,
             norm_mem=norm_mem, w_xq=w_xq, w_xkv=w_xkv, qn_x=qn_x, kn_x=kn_x, w_xo=w_xo,
             norm_ffn=norm_ffn, w_up=w_up, w_down=w_down)
    pos_p = jnp.arange(x_prompt.shape[1])
    pos_s = PAST_LEN + jnp.arange(x_sample.shape[1])
    xp, xs = x_prompt, x_sample
    st_p, st_s, mk_p, mv_p = [], [], [], []
    for l in range(DEPTH):
        mk, mv = _mem_kv(mem_prompt, l, P)
        mk_p.append(mk)
        mv_p.append(mv)
        xp, sp = _layer(xp, pos_p, l, P, mk, mv, None)
        xs, ss = _layer(xs, pos_s, l, P, cache_mem_k[l], cache_mem_v[l],
                        (cache_win_k[l], cache_win_v[l], cache_band_k[l], cache_band_v[l],
                         state_ret[l], state_gla[l]))
        st_p.append(sp)
        st_s.append(ss)

    def stk(states, i):
        return jnp.stack([s[i] for s in states])

    return (xp, xs,
            stk(st_p, 0), stk(st_p, 1), stk(st_p, 2), stk(st_p, 3), stk(st_p, 4), stk(st_p, 5),
            jnp.stack(mk_p), jnp.stack(mv_p),
            stk(st_s, 0), stk(st_s, 1), stk(st_s, 2), stk(st_s, 3), stk(st_s, 4), stk(st_s, 5))
```

```python
import functools

import numpy as np
import jax
import jax.numpy as jnp
from jax import lax
from jax.experimental import pallas as pl
from jax.experimental.pallas import tpu as pltpu

F32 = jnp.float32
BF16 = jnp.bfloat16

EPS = 1e-6
NEG = -1e30
PAST_LEN = 1024
CHUNK = 64
ROPE_THETA = 10000.0
A_PREV = 2 * CHUNK
B_PREV = 8 * CHUNK
REL_CLIP = 128
GLA_RANK = 16
GLA_TAU = 16.0
C_HEADS = 4
LANES = 128
SUB = 8
HD = 64
SEQ_TILE = 256
MIB = 2 ** 20


def _dot(a, b):
    return jnp.dot(a.astype(BF16), b.astype(BF16), preferred_element_type=F32)


def _dot_nt(a, b):
    return lax.dot_general(a.astype(BF16), b.astype(BF16), (((1,), (1,)), ((), ())),
                           preferred_element_type=F32)


def _dot_tn(a, b):
    return lax.dot_general(a.astype(BF16), b.astype(BF16), (((0,), (0,)), ((), ())),
                           preferred_element_type=F32)


def _split3(x):
    x1 = x.astype(BF16)
    r1 = x - x1.astype(F32)
    x2 = r1.astype(BF16)
    x3 = (r1 - x2.astype(F32)).astype(BF16)
    return x1, x2, x3


def _dot_01_left(m01, x):
    x1, x2, x3 = _split3(x)
    dot = functools.partial(jnp.dot, preferred_element_type=F32)
    return dot(m01, x1) + dot(m01, x2) + dot(m01, x3)


def _dot_01_right(x, m01):
    x1, x2, x3 = _split3(x)
    dot = functools.partial(jnp.dot, preferred_element_type=F32)
    return dot(x1, m01) + dot(x2, m01) + dot(x3, m01)


def _rms_rows(x, g):
    return x * lax.rsqrt(jnp.mean(x * x, axis=-1, keepdims=True) + EPS) * g


def _lane_iota(shape):
    return lax.broadcasted_iota(jnp.int32, shape, len(shape) - 1)


def _group_ones():
    r = lax.broadcasted_iota(jnp.int32, (LANES, LANES), 0) // HD
    c = lax.broadcasted_iota(jnp.int32, (LANES, LANES), 1) // HD
    return jnp.where(r == c, 1.0, 0.0).astype(BF16)


def _head64_norm(x, g, ones):
    ss = _dot_01_right(x * x, ones)
    return x * lax.rsqrt(ss * (1.0 / HD) + EPS) * g


def _rope64(x, cos, sin):
    first_half = (_lane_iota((1, LANES)) % HD) < (HD // 2)
    swapped = jnp.where(first_half, pltpu.roll(x, LANES - HD // 2, 1), pltpu.roll(x, HD // 2, 1))
    return x * cos + swapped * sin


def _sigmoid(x):
    return 1.0 / (1.0 + jnp.exp(-x))


def _log_sigmoid(x):
    return jnp.minimum(x, 0.0) - jnp.log(1.0 + jnp.exp(-jnp.abs(x)))


def _params(n_axes, vmem_mib):
    return pltpu.CompilerParams(dimension_semantics=("arbitrary",) * n_axes,
                                vmem_limit_bytes=vmem_mib * MIB)


def _const_spec(shape, index):
    return pl.BlockSpec(shape, lambda *_: index, pipeline_mode=pl.Buffered(1))


def _inproj_kernel(x_ref, g_ref, wa_ref, wb_ref, wc_ref, wd_ref, wup_ref, ba_ref,
                   gqa_ref, gka_ref, gqb_ref, gkb_ref, cos_ref, sin_ref,
                   oa_ref, ob_ref, oc_ref, od_ref):
    h = _rms_rows(x_ref[...], g_ref[...]).astype(BF16)
    cos = cos_ref[...]
    sin = sin_ref[...]
    ones = _group_ones()

    za = jnp.dot(h, wa_ref[...], preferred_element_type=F32)
    for c in range(4):
        sl = slice(c * LANES, (c + 1) * LANES)
        oa_ref[:, sl] = _rope64(_head64_norm(za[:, sl], gqa_ref[:, sl], ones), cos, sin)
    for c in range(2):
        sl = slice(512 + c * LANES, 512 + (c + 1) * LANES)
        oa_ref[:, sl] = _rope64(_head64_norm(za[:, sl], gka_ref[...], ones), cos, sin)
    oa_ref[:, 768:1024] = za[:, 768:1024]

    zb = jnp.dot(h, wb_ref[...], preferred_element_type=F32)
    for c in range(4):
        sl = slice(c * LANES, (c + 1) * LANES)
        ob_ref[:, sl] = _head64_norm(zb[:, sl], gqb_ref[:, sl], ones)
    for c in range(4):
        sl = slice(512 + c * LANES, 512 + (c + 1) * LANES)
        ob_ref[:, sl] = _head64_norm(zb[:, sl], gkb_ref[:, c * LANES:(c + 1) * LANES], ones)
    ob_ref[:, 1024:1536] = zb[:, 1024:1536]

    zc = jnp.dot(h, wc_ref[...], preferred_element_type=F32)
    for c in range(2):
        sl = slice(c * LANES, (c + 1) * LANES)
        oc_ref[:, sl] = _rope64(zc[:, sl], cos, sin)
    for c in range(2):
        sl = slice(256 + c * LANES, 256 + (c + 1) * LANES)
        oc_ref[:, sl] = _rope64(zc[:, sl], cos, sin) * (HD ** -0.5)
    oc_ref[:, 512:1536] = zc[:, 512:1536]

    zd = jnp.dot(h, wd_ref[...], preferred_element_type=F32)
    od_ref[:, 0:256] = zd[:, 0:256] * (HD ** -0.5)
    od_ref[:, 256:1536] = zd[:, 256:1536]
    pre = _dot(zd[:, 1536:1664], wup_ref[...]) + ba_ref[...]
    od_ref[:, 1536:1792] = _log_sigmoid(pre) * (1.0 / GLA_TAU)


def _inproj(x, l, W, cos, sin, n_pos_blocks, tm):
    n, d = x.shape
    row = lambda i: (i, 0)
    lay = (l, 0, 0)
    in_specs = [
        pl.BlockSpec((tm, d), row),
        _const_spec((None, 1, d), lay),
        _const_spec((None, d, 1024), lay),
        _const_spec((None, d, 1536), lay),
        _const_spec((None, d, 1536), lay),
        _const_spec((None, d, 1664), lay),
        _const_spec((None, LANES, 256), lay),
        _const_spec((None, 1, 256), lay),
        _const_spec((None, 1, 512), lay),
        _const_spec((None, 1, LANES), lay),
        _const_spec((None, 1, 512), lay),
        _const_spec((None, 1, 512), lay),
        pl.BlockSpec((tm, LANES), lambda i: (i % n_pos_blocks, 0)),
        pl.BlockSpec((tm, LANES), lambda i: (i % n_pos_blocks, 0)),
    ]
    widths = (1024, 1536, 1536, 1792)
    return pl.pallas_call(
        _inproj_kernel,
        grid=(n // tm,),
        in_specs=in_specs,
        out_specs=[pl.BlockSpec((tm, w), row) for w in widths],
        out_shape=[jax.ShapeDtypeStruct((n, w), F32) for w in widths],
        compiler_params=_params(1, 56),
        name="inproj",
    )(x, W["norm_mix"], W["wa"], W["wb"], W["wc"], W["wd"], W["wup"], W["b_alpha"],
      W["gqa"], W["gka"], W["gqb"], W["gkb"], cos, sin)


def _attn_kernel(*refs, n_kv, n_qcol, k_col_of_q, has_bias, bias_per_head, has_sink,
                 masked_prefix):
    q_ref = refs[0]
    k_refs = refs[1:1 + n_kv]
    v_refs = refs[1 + n_kv:1 + 2 * n_kv]
    pos = 1 + 2 * n_kv
    bias_ref = None
    sink_ref = None
    if has_bias:
        bias_ref = refs[pos]
        pos += 1
    if has_sink:
        sink_ref = refs[pos]
        pos += 1
    o_ref = refs[pos]

    k_all = jnp.concatenate([r[...].astype(BF16) for r in k_refs], axis=0)
    v_all = jnp.concatenate([r[...].astype(BF16) for r in v_refs], axis=0)
    tq = q_ref.shape[0]
    nk = k_all.shape[0]
    low = _lane_iota((1, LANES)) < HD
    if masked_prefix is not None:
        prev_rows, seq_tile = masked_prefix
        first_valid = prev_rows - pl.program_id(1) * seq_tile
        key_ok = _lane_iota((1, nk)) >= first_valid

    for c in range(n_qcol):
        sl = slice(c * LANES, (c + 1) * LANES)
        kc = k_col_of_q(c)
        k_c = k_all[:, kc * LANES:(kc + 1) * LANES]
        v_c = v_all[:, kc * LANES:(kc + 1) * LANES]
        q_c = q_ref[:, sl]
        halves = []
        for half in range(2):
            head = 2 * c + half
            q_h = jnp.where(low if half == 0 else jnp.logical_not(low), q_c, 0.0)
            s = _dot_nt(q_h, k_c)
            if has_bias:
                s = s + bias_ref[head if bias_per_head else 0]
            if masked_prefix is not None:
                s = jnp.where(key_ok, s, NEG)
            m = jnp.max(s, axis=-1, keepdims=True)
            if has_sink:
                sink = sink_ref[head:head + 1, 0:1]
                m = jnp.maximum(m, sink)
            p = jnp.exp(s - m)
            den = jnp.sum(p, axis=-1, keepdims=True)
            if has_sink:
                den = den + jnp.exp(sink - m)
            halves.append(_dot(p, v_c) / den)
        o_ref[:, sl] = jnp.where(low, halves[0], halves[1])


def _attention(q_arr, q_col0, n_qcol, kv_arrs, k_col_of_q, bias, bias_per_head, sink,
               grid, q_rows, q_index, masked_prefix, vmem_mib, name):
    in_specs = [pl.BlockSpec((q_rows, n_qcol * LANES),
                             lambda b, i: (q_index(b, i), q_col0 // (n_qcol * LANES)))]
    args = [q_arr]
    for which in (1, 2):
        for arr, rows, kc0, vc0, ncol, index_fn in kv_arrs:
            col0 = kc0 if which == 1 else vc0
            width = ncol * LANES
            in_specs.append(pl.BlockSpec(
                (rows, width), functools.partial(lambda b, i, f, cb: (f(b, i), cb), f=index_fn,
                                                 cb=col0 // width)))
            args.append(arr)
    if bias is not None:
        nb, _, nk = bias.shape
        in_specs.append(_const_spec((nb, q_rows, nk), (0, 0, 0)))
        args.append(bias)
    if sink is not None:
        in_specs.append(_const_spec(sink.shape, (0, 0)))
        args.append(sink)
    n_rows = grid[0] * grid[1] * q_rows
    kern = functools.partial(
        _attn_kernel, n_kv=len(kv_arrs), n_qcol=n_qcol, k_col_of_q=k_col_of_q,
        has_bias=bias is not None, bias_per_head=bias_per_head, has_sink=sink is not None,
        masked_prefix=masked_prefix)
    return pl.pallas_call(
        kern,
        grid=grid,
        in_specs=in_specs,
        out_specs=pl.BlockSpec((q_rows, n_qcol * LANES), lambda b, i: (q_index(b, i), 0)),
        out_shape=jax.ShapeDtypeStruct((n_rows, n_qcol * LANES), F32),
        compiler_params=_params(2, vmem_mib),
        name=name,
    )(*args)


def _head_lane_mask(head):
    low = _lane_iota((1, LANES)) < HD
    return low if head % 2 == 0 else jnp.logical_not(low)


def _ret_kernel(z_ref, dmat_ref, qdec_ref, kdec_ref, sdec_ref, gn_ref, s0_ref,
                o_ref, sout_ref, st_ref):
    i = pl.program_id(1)

    @pl.when(i == 0)
    def _():
        st_ref[...] = s0_ref[...]

    q = z_ref[:, 0:256]
    k = z_ref[:, 256:512]
    q_in = q * qdec_ref[...]
    k_st = k * kdec_ref[...]
    for h in range(C_HEADS):
        p = h // 2
        pair = slice(p * LANES, (p + 1) * LANES)
        hm = _head_lane_mask(h)
        v_h = z_ref[:, 512 + h * LANES:512 + (h + 1) * LANES]
        g_h = z_ref[:, 1024 + h * LANES:1024 + (h + 1) * LANES]
        k_h = jnp.where(hm, k[:, pair], 0.0)
        scores = _dot_nt(q[:, pair], k_h) * dmat_ref[h]
        st = st_ref[h]
        o = _dot(scores, v_h) + _dot_nt(q_in[:, pair], st)
        oc = o - jnp.mean(o, axis=-1, keepdims=True)
        y = oc * lax.rsqrt(jnp.mean(oc * oc, axis=-1, keepdims=True) + EPS)
        y = y * gn_ref[:, h * LANES:(h + 1) * LANES]
        o_ref[:, h * LANES:(h + 1) * LANES] = y * (g_h * _sigmoid(g_h))
        upd = _dot_tn(v_h, jnp.where(hm, k_st[:, pair], 0.0))
        st_ref[h] = st * sdec_ref[:, pair] + upd

    @pl.when(i == pl.num_programs(1) - 1)
    def _():
        sout_ref[...] = st_ref[...]


def _gla_kernel(z_ref, tril_ref, lmask_ref, bd_ref, ind_ref, rep_ref, gn_ref, s0_ref,
                o_ref, sout_ref, st_ref, *, levels):
    i = pl.program_id(1)

    @pl.when(i == 0)
    def _():
        st_ref[...] = s0_ref[...]

    tc = z_ref.shape[0]
    q = z_ref[:, 0:256]
    k = z_ref[:, 256:512]
    la = z_ref[:, 1536:1792]
    cum = _dot_01_left(tril_ref[...], la)
    cum_last = cum[tc - 1:tc, :]

    heads = range(C_HEADS)
    scores = [None] * C_HEADS
    for lv, size in enumerate(levels):
        mids = []
        for start in range(0, tc, size):
            m = start + size // 2 - 1
            mids.append(jnp.broadcast_to(cum[m:m + 1, :], (size, 256)))
        cm = mids[0] if len(mids) == 1 else jnp.concatenate(mids, axis=0)
        q_l = q * jnp.exp(jnp.minimum(cum - cm, 0.0))
        k_l = k * jnp.exp(jnp.minimum(cm - cum, 0.0))
        lm = lmask_ref[lv]
        for h in heads:
            pair = slice((h // 2) * LANES, (h // 2 + 1) * LANES)
            k_h = jnp.where(_head_lane_mask(h), k_l[:, pair], 0.0)
            part = _dot_nt(q_l[:, pair], k_h) * lm
            scores[h] = part if scores[h] is None else scores[h] + part

    nb = tc // SUB
    q3 = q.reshape(nb, SUB, 256)
    k3 = k.reshape(nb, SUB, 256)
    c3 = cum.reshape(nb, SUB, 256)
    row_in_block = lax.broadcasted_iota(jnp.int32, (1, SUB, 1), 1)
    r = None
    for j in range(SUB):
        e = jnp.exp(jnp.minimum(c3 - c3[:, j:j + 1, :], 0.0))
        t = jnp.where(row_in_block >= j, q3 * k3[:, j:j + 1, :] * e, 0.0)
        part = jnp.dot(t.reshape(tc, 256).astype(BF16), ind_ref[j], preferred_element_type=F32)
        r = part if r is None else r + part
    r = r.astype(BF16)
    bd = bd_ref[...]

    q_in = q * jnp.exp(cum)
    k_st = k * jnp.exp(cum_last - cum)
    s_dec = jnp.exp(cum_last)
    for h in heads:
        p = h // 2
        pair = slice(p * LANES, (p + 1) * LANES)
        hm = _head_lane_mask(h)
        v_h = z_ref[:, 512 + h * LANES:512 + (h + 1) * LANES]
        g_h = z_ref[:, 1024 + h * LANES:1024 + (h + 1) * LANES]
        s_h = scores[h] + jnp.dot(r, rep_ref[h], preferred_element_type=F32) * bd
        st = st_ref[h]
        o = _dot(s_h, v_h) + _dot_nt(q_in[:, pair], st)
        y = o * lax.rsqrt(jnp.mean(o * o, axis=-1, keepdims=True) + EPS)
        y = y * gn_ref[:, h * LANES:(h + 1) * LANES]
        o_ref[:, h * LANES:(h + 1) * LANES] = y * (g_h * _sigmoid(g_h))
        upd = _dot_tn(v_h, jnp.where(hm, k_st[:, pair], 0.0))
        st_ref[h] = st * s_dec[:, pair] + upd

    @pl.when(i == pl.num_programs(1) - 1)
    def _():
        sout_ref[...] = st_ref[...]


def _ret_constants(tc):
    log_gamma = np.log1p(-np.exp2(-5.0 - np.arange(C_HEADS, dtype=np.float64)))
    idx = np.arange(tc)
    diff = idx[:, None] - idx[None, :]
    dmat = np.where(diff >= 0, np.exp(log_gamma[:, None, None] * np.maximum(diff, 0)[None]), 0.0)
    per_lane = np.repeat(log_gamma, HD)
    qdec = np.exp((idx[:, None] + 1) * per_lane[None, :])
    kdec = np.exp((tc - 1 - idx[:, None]) * per_lane[None, :])
    sdec = np.exp(tc * per_lane)[None, :]
    return [jnp.asarray(a, F32) for a in (dmat, qdec, kdec, sdec)]


def _gla_levels(tc):
    levels = []
    size = tc
    while size >= 2 * SUB:
        levels.append(size)
        size //= 2
    return tuple(levels)


def _gla_constants(tc):
    idx = np.arange(tc)
    i, j = idx[:, None], idx[None, :]
    tril = (j <= i).astype(np.float32)
    lmasks = []
    for size in _gla_levels(tc):
        lmasks.append(((i // size == j // size) & (i % size >= size // 2) & (j % size < size // 2)))
    lmask = np.stack(lmasks).astype(np.float32)
    bd = (i // SUB == j // SUB).astype(np.float32)
    ind = np.zeros((SUB, 256, LANES), np.float32)
    for jj in range(SUB):
        for h in range(C_HEADS):
            ind[jj, h * HD:(h + 1) * HD, h * SUB + jj] = 1.0
    rep = np.zeros((C_HEADS, LANES, tc), np.float32)
    for h in range(C_HEADS):
        for jj in range(SUB):
            rep[h, h * SUB + jj, jj::SUB] = 1.0
    return (jnp.asarray(tril, BF16), jnp.asarray(lmask, F32), jnp.asarray(bd, F32),
            jnp.asarray(ind, BF16), jnp.asarray(rep, BF16))


def _recurrent_specs(tc, n_tiles, width):
    z_spec = pl.BlockSpec((tc, width), lambda b, i: (b * n_tiles + i, 0))
    o_spec = pl.BlockSpec((tc, 512), lambda b, i: (b * n_tiles + i, 0))
    s_spec = pl.BlockSpec((None, C_HEADS, LANES, LANES), lambda b, i: (b, 0, 0, 0))
    return z_spec, o_spec, s_spec


def _retention(zc, l, gn, s0, batch, seq, tc):
    n_tiles = seq // tc
    consts = _ret_constants(tc)
    z_spec, o_spec, s_spec = _recurrent_specs(tc, n_tiles, 1536)
    in_specs = [z_spec,
                _const_spec((C_HEADS, tc, tc), (0, 0, 0)),
                _const_spec((tc, 256), (0, 0)),
                _const_spec((tc, 256), (0, 0)),
                _const_spec((1, 256), (0, 0)),
                _const_spec((None, 1, 512), (l, 0, 0)),
                s_spec]
    return pl.pallas_call(
        _ret_kernel,
        grid=(batch, n_tiles),
        in_specs=in_specs,
        out_specs=[o_spec, s_spec],
        out_shape=[jax.ShapeDtypeStruct((batch * seq, 512), F32),
                   jax.ShapeDtypeStruct((batch, C_HEADS, LANES, LANES), F32)],
        scratch_shapes=[pltpu.VMEM((C_HEADS, LANES, LANES), F32)],
        compiler_params=_params(2, 40),
        name="retention",
    )(zc, *consts, gn, s0)


def _gla(zd, l, gn, s0, batch, seq, tc):
    n_tiles = seq // tc
    levels = _gla_levels(tc)
    tril, lmask, bd, ind, rep = _gla_constants(tc)
    z_spec, o_spec, s_spec = _recurrent_specs(tc, n_tiles, 1792)
    in_specs = [z_spec,
                _const_spec((tc, tc), (0, 0)),
                _const_spec((len(levels), tc, tc), (0, 0, 0)),
                _const_spec((tc, tc), (0, 0)),
                _const_spec((SUB, 256, LANES), (0, 0, 0)),
                _const_spec((C_HEADS, LANES, tc), (0, 0, 0)),
                _const_spec((None, 1, 512), (l, 0, 0)),
                s_spec]
    return pl.pallas_call(
        functools.partial(_gla_kernel, levels=levels),
        grid=(batch, n_tiles),
        in_specs=in_specs,
        out_specs=[o_spec, s_spec],
        out_shape=[jax.ShapeDtypeStruct((batch * seq, 512), F32),
                   jax.ShapeDtypeStruct((batch, C_HEADS, LANES, LANES), F32)],
        scratch_shapes=[pltpu.VMEM((C_HEADS, LANES, LANES), F32)],
        compiler_params=_params(2, 48),
        name="gla",
    )(zd, tril, lmask, bd, ind, rep, gn, s0)


def _state_to_kernel(s):
    st = jnp.swapaxes(s, -1, -2)
    lo = jnp.pad(st, ((0, 0), (0, 0), (0, 0), (0, HD)))
    hi = jnp.pad(st, ((0, 0), (0, 0), (0, 0), (HD, 0)))
    odd = (jnp.arange(C_HEADS) % 2 == 1)[None, :, None, None]
    return jnp.where(odd, hi, lo)


def _state_from_kernel(st):
    lo = st[..., :HD]
    hi = st[..., HD:]
    odd = (jnp.arange(C_HEADS) % 2 == 1)[None, :, None, None]
    return jnp.swapaxes(jnp.where(odd, hi, lo), -1, -2)


def _merge_kernel(x_ref, g_ref, oa_ref, ob_ref, oc_ref, od_ref, wm_ref, wbr_ref, wo_ref, y_ref):
    x = x_ref[...]
    d = x.shape[1]
    h = _rms_rows(x, g_ref[...]).astype(BF16)
    acc = None
    for b, o_ref in enumerate((oa_ref, ob_ref, oc_ref, od_ref)):
        gate = _sigmoid(jnp.dot(h, wm_ref[:, b * d:(b + 1) * d], preferred_element_type=F32))
        term = gate * _dot(o_ref[...], wbr_ref[b])
        acc = term if acc is None else acc + term
    y_ref[...] = x + _dot(acc, wo_ref[...])


def _merge(x, outs, l, W, tm):
    n, d = x.shape
    row = lambda i: (i, 0)
    lay = (l, 0, 0)
    in_specs = ([pl.BlockSpec((tm, d), row), _const_spec((None, 1, d), lay)]
                + [pl.BlockSpec((tm, 512), row)] * 4
                + [_const_spec((None, d, 4 * d), lay),
                   _const_spec((None, 4, 512, d), (l, 0, 0, 0)),
                   _const_spec((None, d, d), lay)])
    return pl.pallas_call(
        _merge_kernel,
        grid=(n // tm,),
        in_specs=in_specs,
        out_specs=pl.BlockSpec((tm, d), row),
        out_shape=jax.ShapeDtypeStruct((n, d), F32),
        compiler_params=_params(1, 48),
        name="merge",
    )(x, W["norm_mix"], *outs, W["w_merge"], W["w_branch"], W["w_out"])


def _cross_ffn_kernel(x_ref, gx_ref, wq_ref, qn_ref, mk_ref, mv_ref, wxo_ref, gf_ref, wup_ref,
                      wdn_ref, y_ref):
    x = x_ref[...]
    d = x.shape[1]
    hx = _rms_rows(x, gx_ref[...])
    q = _dot(hx, wq_ref[...])
    outs = []
    for h in range(4):
        sl = slice(h * LANES, (h + 1) * LANES)
        q_h = _rms_rows(q[:, sl], qn_ref[...]) * (LANES ** -0.5)
        s = _dot_nt(q_h, mk_ref[:, sl])
        p = jnp.exp(s - jnp.max(s, axis=-1, keepdims=True))
        outs.append(_dot(p, mv_ref[:, sl]) / jnp.sum(p, axis=-1, keepdims=True))
    x = x + _dot(jnp.concatenate(outs, axis=1), wxo_ref[...])
    hf = _rms_rows(x, gf_ref[...]).astype(BF16)
    acc = None
    for j in range(wup_ref.shape[1] // d):
        sl = slice(j * d, (j + 1) * d)
        u = jnp.maximum(jnp.dot(hf, wup_ref[:, sl], preferred_element_type=F32), 0.0)
        term = _dot(u * u, wdn_ref[sl, :])
        acc = term if acc is None else acc + term
    y_ref[...] = x + acc


def _cross_ffn(x, mk, mv, l, W, grid, rows, x_index, mem_index):
    n, d = x.shape
    mlen = mk.shape[0] // (grid[0] if mem_index is None else 1)
    lay = (l, 0, 0)
    dff = W["w_up"].shape[-1]
    in_specs = [pl.BlockSpec((rows, d), lambda b, i: (x_index(b, i), 0)),
                _const_spec((None, 1, d), lay),
                _const_spec((None, d, 512), lay),
                _const_spec((None, 1, LANES), lay),
                pl.BlockSpec((mlen, 512), lambda b, i: (b, 0)),
                pl.BlockSpec((mlen, 512), lambda b, i: (b, 0)),
                _const_spec((None, 512, d), lay),
                _const_spec((None, 1, d), lay),
                _const_spec((None, d, dff), lay),
                _const_spec((None, dff, d), lay)]
    return pl.pallas_call(
        _cross_ffn_kernel,
        grid=grid,
        in_specs=in_specs,
        out_specs=pl.BlockSpec((rows, d), lambda b, i: (x_index(b, i), 0)),
        out_shape=jax.ShapeDtypeStruct((n, d), F32),
        compiler_params=_params(2, 48),
        name="cross_ffn",
    )(x, W["norm_x"], W["w_xq"], W["qn_x"], mk, mv, W["w_xo"], W["norm_ffn"], W["w_up"],
      W["w_down"])


def _memkv_kernel(mem_ref, g_ref, w_ref, kn_ref, k_ref, v_ref):
    hm = _rms_rows(mem_ref[...], g_ref[...])
    kv = _dot(hm, w_ref[...])
    for h in range(4):
        sl = slice(h * LANES, (h + 1) * LANES)
        k_ref[:, sl] = _rms_rows(kv[:, sl], kn_ref[...])
    v_ref[...] = kv[:, 512:1024]


def _memkv(mem, W, depth):
    n, d = mem.shape
    mlen = 256
    batch = n // mlen
    lay = lambda l, b: (l, 0, 0)
    return pl.pallas_call(
        _memkv_kernel,
        grid=(depth, batch),
        in_specs=[pl.BlockSpec((mlen, d), lambda l, b: (b, 0)),
                  pl.BlockSpec((None, 1, d), lay),
                  pl.BlockSpec((None, d, 1024), lay),
                  pl.BlockSpec((None, 1, LANES), lay)],
        out_specs=[pl.BlockSpec((None, mlen, 512), lambda l, b: (l, b, 0))] * 2,
        out_shape=[jax.ShapeDtypeStruct((depth, n, 512), F32)] * 2,
        compiler_params=_params(2, 32),
        name="memkv",
    )(mem, W["norm_mem"], W["w_xkv"], W["kn_x"])


def _rope_tables(pos):
    half = HD // 2
    freqs = ROPE_THETA ** (-jnp.arange(half, dtype=F32) / half)
    ang = pos.astype(F32)[:, None] * freqs[None, :]
    cos = jnp.tile(jnp.cos(ang), (1, LANES // half))
    sin = jnp.sin(ang)
    sin = jnp.tile(jnp.concatenate([-sin, sin], axis=1), (1, LANES // HD))
    return cos, sin


def _dup_heads(w, col0, n_heads):
    lead = w.shape[:-1]
    blk = w[..., col0:col0 + n_heads * HD].reshape(*lead, n_heads, 1, HD)
    return jnp.broadcast_to(blk, (*lead, n_heads, 2, HD)).reshape(*lead, n_heads * 2 * HD)


def _prepare_weights(P):
    depth, d, _ = P["w_in"].shape
    w_in = P["w_in"]
    vec = lambda a: a.reshape(depth, 1, -1).astype(F32)
    W = {}
    W["wa"] = jnp.concatenate([w_in[..., 0:512], _dup_heads(w_in, 512, 2), _dup_heads(w_in, 640, 2)],
                              axis=-1).astype(BF16)
    W["wb"] = w_in[..., 768:2304].astype(BF16)
    W["wc"] = w_in[..., 2304:3840].astype(BF16)
    W["wd"] = jnp.pad(w_in[..., 3840:5392], ((0, 0), (0, 0), (0, LANES - GLA_RANK))).astype(BF16)
    W["wup"] = jnp.pad(P["w_alpha_up"], ((0, 0), (0, LANES - GLA_RANK), (0, 0))).astype(BF16)
    W["b_alpha"] = vec(P["b_alpha"])
    W["norm_mix"] = vec(P["norm_mix"])
    W["gqa"] = vec(jnp.tile(P["qn_a"], (1, 8)) * (HD ** -0.5))
    W["gka"] = vec(jnp.tile(P["kn_a"], (1, 2)))
    W["gqb"] = vec(jnp.tile(P["qn_b"], (1, 8)) * (HD ** -0.5))
    W["gkb"] = vec(jnp.tile(P["kn_b"], (1, 8)))
    W["gn_c"] = vec(P["gn_c"])
    W["gn_d"] = vec(P["gn_d"])
    W["sink"] = jnp.broadcast_to(P["sink_a"].astype(F32)[:, :, None], (depth, 8, LANES))
    for name in ("w_merge", "w_branch", "w_out", "w_xq", "w_xkv", "w_xo", "w_up", "w_down"):
        W[name] = P[name].astype(BF16)
    for name in ("norm_x", "norm_mem", "qn_x", "kn_x", "norm_ffn"):
        W[name] = vec(P[name])
    return W


def _band_mask(q_rows, prev_rows, n_prev_chunks):
    r = np.arange(q_rows)[:, None] // CHUNK
    u = np.arange(prev_rows + q_rows)[None, :] // CHUNK - prev_rows // CHUNK
    return (u <= r) & (u >= r - n_prev_chunks)


def _rel_bias(table, q_rows, prev_rows):
    r = np.arange(q_rows)[:, None]
    u = np.arange(prev_rows + q_rows)[None, :]
    idx = np.clip(r + prev_rows - u, -REL_CLIP, REL_CLIP) + REL_CLIP
    return table[:, idx]


def _mixers_prompt(slabs, l, W, relbias, batch, seq):
    za, zb, zc, zd = slabs
    tq = SEQ_TILE
    n_tiles = seq // tq
    cur = lambda b, i: b * n_tiles + i

    def prev(rows, back):
        per_tile = tq // rows
        return lambda b, i: jnp.maximum(b * n_tiles * per_tile + i * per_tile - back,
                                        b * n_tiles * per_tile)

    mask_a = jnp.asarray(np.where(_band_mask(tq, A_PREV, 2), 0.0, NEG)[None], F32)
    oa = _attention(
        za, 0, 4,
        [(za, A_PREV, 512, 768, 2, prev(A_PREV, 1)), (za, tq, 512, 768, 2, cur)],
        lambda c: c // 2, mask_a, False, W["sink"][l], (batch, n_tiles), tq, cur,
        (A_PREV, tq), 32, "attn_a")

    bias_b = _rel_bias(relbias, tq, B_PREV) + jnp.asarray(
        np.where(_band_mask(tq, B_PREV, 8), 0.0, NEG)[None], F32)
    n_prev_b = B_PREV // tq
    kv_b = [(zb, tq, 512, 1024, 4, prev(tq, n_prev_b - j)) for j in range(n_prev_b)]
    kv_b.append((zb, tq, 512, 1024, 4, cur))
    ob = _attention(zb, 0, 4, kv_b, lambda c: c, bias_b, True, None, (batch, n_tiles), tq, cur,
                    (B_PREV, tq), 56, "attn_b")

    zero_state = jnp.zeros((batch, C_HEADS, LANES, LANES), F32)
    oc, sc = _retention(zc, l, W["gn_c"], zero_state, batch, seq, tq)
    od, sd = _gla(zd, l, W["gn_d"], zero_state, batch, seq, tq)
    return (oa, ob, oc, od), (sc, sd)


def _mixers_sample(slabs, l, W, relbias, cache, batch, seq):
    za, zb, zc, zd = slabs
    win_k, win_v, band_k, band_v, s_c0, s_d0 = cache
    cur = lambda b, i: b
    la = win_k.shape[1]
    lb = band_k.shape[1]
    ck = jnp.broadcast_to(win_k[:, :, :, None, :], (batch, la, 2, 2, HD)).reshape(batch * la, 256)
    cv = jnp.broadcast_to(win_v[:, :, :, None, :], (batch, la, 2, 2, HD)).reshape(batch * la, 256)
    cache_a = jnp.concatenate([ck, cv], axis=1)
    oa = _attention(
        za, 0, 4,
        [(cache_a, la, 0, 256, 2, cur), (za, seq, 512, 768, 2, cur)],
        lambda c: c // 2, None, False, W["sink"][l], (batch, 1), seq, cur, None, 32, "attn_a_step")

    cache_b = jnp.concatenate([band_k.reshape(batch * lb, 512), band_v.reshape(batch * lb, 512)],
                              axis=1)
    bias_b = _rel_bias(relbias, seq, lb)
    ob = _attention(
        zb, 0, 4,
        [(cache_b, lb, 0, 512, 4, cur), (zb, seq, 512, 1024, 4, cur)],
        lambda c: c, bias_b, True, None, (batch, 1), seq, cur, None, 32, "attn_b_step")

    oc, sc = _retention(zc, l, W["gn_c"], _state_to_kernel(s_c0), batch, seq, seq)
    od, sd = _gla(zd, l, W["gn_d"], _state_to_kernel(s_d0), batch, seq, seq)
    return (oa, ob, oc, od), (sc, sd)


def _new_rows(slabs, batch, seq, rows_a, rows_b):
    za, zb = slabs[0], slabs[1]
    ka = za[:, 512:768].reshape(batch, seq, 2, 2, HD)[:, seq - rows_a:, :, 0, :]
    va = za[:, 768:1024].reshape(batch, seq, 2, 2, HD)[:, seq - rows_a:, :, 0, :]
    kb = zb[:, 512:1024].reshape(batch, seq, 8, HD)[:, seq - rows_b:]
    vb = zb[:, 1024:1536].reshape(batch, seq, 8, HD)[:, seq - rows_b:]
    return ka, va, kb, vb


def kernel(x_prompt, x_sample, cache_win_k, cache_win_v, cache_band_k, cache_band_v, state_ret, state_gla, cache_mem_k, cache_mem_v, mem_prompt, norm_mix, w_in, qn_a, kn_a, sink_a, qn_b, kn_b, relbias_b, gn_c, w_alpha_up, b_alpha, gn_d, w_branch, w_merge, w_out, norm_x, norm_mem, w_xq, w_xkv, qn_x, kn_x, w_xo, norm_ffn, w_up, w_down):
    P = dict(norm_mix=norm_mix, w_in=w_in, qn_a=qn_a, kn_a=kn_a, sink_a=sink_a, qn_b=qn_b,
             kn_b=kn_b, relbias_b=relbias_b, gn_c=gn_c, w_alpha_up=w_alpha_up, b_alpha=b_alpha,
             gn_d=gn_d, w_branch=w_branch, w_merge=w_merge, w_out=w_out, norm_x=norm_x,
             norm_mem=norm_mem, w_xq=w_xq, w_xkv=w_xkv, qn_x=qn_x, kn_x=kn_x, w_xo=w_xo,
             norm_ffn=norm_ffn, w_up=w_up, w_down=w_down)
    depth = w_in.shape[0]
    bp, tp, d = x_prompt.shape
    bs, ts, _ = x_sample.shape
    mlen = mem_prompt.shape[1]
    W = _prepare_weights(P)

    cos_p, sin_p = _rope_tables(jnp.arange(tp))
    cos_s, sin_s = _rope_tables(jnp.tile(PAST_LEN + jnp.arange(ts), bs))
    mem_k, mem_v = _memkv(mem_prompt.reshape(bp * mlen, d), W, depth)

    xp = x_prompt.reshape(bp * tp, d)
    xs = x_sample.reshape(bs * ts, d)
    tm = SEQ_TILE
    n_tiles = tp // tm
    rows_p, rows_s = [], []
    st_p, st_s = [], []
    for l in range(depth):
        relbias = relbias_b[l].astype(F32)
        slabs = _inproj(xp, l, W, cos_p, sin_p, tp // tm, tm)
        outs, states = _mixers_prompt(slabs, l, W, relbias, bp, tp)
        rows_p.append(_new_rows(slabs, bp, tp, min(A_PREV, tp), min(B_PREV, tp)))
        st_p.append(states)
        xp = _merge(xp, outs, l, W, tm)
        xp = _cross_ffn(xp, mem_k[l], mem_v[l], l, W, (bp, n_tiles), tm,
                        lambda b, i: b * n_tiles + i, None)
        slabs = _inproj(xs, l, W, cos_s, sin_s, 1, bs * ts)
        cache = (cache_win_k[l], cache_win_v[l], cache_band_k[l], cache_band_v[l],
                 state_ret[l], state_gla[l])
        outs, states = _mixers_sample(slabs, l, W, relbias, cache, bs, ts)
        rows_s.append(_new_rows(slabs, bs, ts, ts, ts))
        st_s.append(states)
        xs = _merge(xs, outs, l, W, bs * ts)
        xs = _cross_ffn(xs, cache_mem_k[l].reshape(bs * mlen, 512),
                        cache_mem_v[l].reshape(bs * mlen, 512), l, W, (bs, 1), ts,
                        lambda b, i: b, None)

    def stack_rows(rows, k):
        return jnp.stack([r[k] for r in rows])

    def stack_state(states, k):
        return jnp.stack([_state_from_kernel(s[k]) for s in states])

    return (xp.reshape(bp, tp, d), xs.reshape(bs, ts, d),
            stack_rows(rows_p, 0), stack_rows(rows_p, 1), stack_rows(rows_p, 2), stack_rows(rows_p, 3),
            stack_state(st_p, 0), stack_state(st_p, 1),
            mem_k.reshape(depth, bp, mlen, 4, LANES), mem_v.reshape(depth, bp, mlen, 4, LANES),
            stack_rows(rows_s, 0), stack_rows(rows_s, 1), stack_rows(rows_s, 2), stack_rows(rows_s, 3),
            stack_state(st_s, 0), stack_state(st_s, 1))
```

```python
import functools

import numpy as np
import jax
import jax.numpy as jnp
from jax import lax
from jax.experimental import pallas as pl
from jax.experimental.pallas import tpu as pltpu

F32 = jnp.float32
BF16 = jnp.bfloat16

EPS = 1e-6
NEG = -1e30
LOG2E = 1.4426950408889634
PAST_LEN = 1024
CHUNK = 64
ROPE_THETA = 10000.0
A_PREV = 2 * CHUNK
B_PREV = 8 * CHUNK
REL_CLIP = 128
GLA_RANK = 16
GLA_TAU = 16.0
C_HEADS = 4
LANES = 128
SUB = 8
HD = 64
SEQ_TILE = 256
MIB = 2 ** 20


def _dot(a, b):
    return jnp.dot(a.astype(BF16), b.astype(BF16), preferred_element_type=F32)


def _dot_nt(a, b):
    return lax.dot_general(a.astype(BF16), b.astype(BF16), (((1,), (1,)), ((), ())),
                           preferred_element_type=F32)


def _dot_tn(a, b):
    return lax.dot_general(a.astype(BF16), b.astype(BF16), (((0,), (0,)), ((), ())),
                           preferred_element_type=F32)


def _split3(x):
    x1 = x.astype(BF16)
    r1 = x - x1.astype(F32)
    x2 = r1.astype(BF16)
    x3 = (r1 - x2.astype(F32)).astype(BF16)
    return x1, x2, x3


def _dot_01_left(m01, x):
    x1, x2, x3 = _split3(x)
    dot = functools.partial(jnp.dot, preferred_element_type=F32)
    return dot(m01, x1) + dot(m01, x2) + dot(m01, x3)


def _dot_01_right(x, m01):
    x1, x2, x3 = _split3(x)
    dot = functools.partial(jnp.dot, preferred_element_type=F32)
    return dot(x1, m01) + dot(x2, m01) + dot(x3, m01)


def _rms_rows(x, g):
    return x * lax.rsqrt(jnp.mean(x * x, axis=-1, keepdims=True) + EPS) * g


def _lane_iota(shape):
    return lax.broadcasted_iota(jnp.int32, shape, len(shape) - 1)


def _group_ones():
    r = lax.broadcasted_iota(jnp.int32, (LANES, LANES), 0) // HD
    c = lax.broadcasted_iota(jnp.int32, (LANES, LANES), 1) // HD
    return jnp.where(r == c, 1.0, 0.0).astype(BF16)


def _head64_norm(x, g, ones):
    ss = _dot_01_right(x * x, ones)
    return x * lax.rsqrt(ss * (1.0 / HD) + EPS) * g


def _rope64(x, cos, sin):
    first_half = (_lane_iota((1, LANES)) % HD) < (HD // 2)
    swapped = jnp.where(first_half, pltpu.roll(x, LANES - HD // 2, 1), pltpu.roll(x, HD // 2, 1))
    return x * cos + swapped * sin


def _sigmoid(x):
    return 1.0 / (1.0 + jnp.exp(-x))


def _log_sigmoid(x):
    return jnp.minimum(x, 0.0) - jnp.log(1.0 + jnp.exp(-jnp.abs(x)))


def _params(n_axes, vmem_mib):
    return pltpu.CompilerParams(dimension_semantics=("arbitrary",) * n_axes,
                                vmem_limit_bytes=vmem_mib * MIB)


def _const_spec(shape, index):
    return pl.BlockSpec(shape, lambda *_: index, pipeline_mode=pl.Buffered(1))


def _inproj_kernel(x_ref, g_ref, wa_ref, wb_ref, wc_ref, wd_ref, wup_ref, ba_ref,
                   gqa_ref, gka_ref, gqb_ref, gkb_ref, cos_ref, sin_ref,
                   oa_ref, ob_ref, oc_ref, od_ref):
    h = _rms_rows(x_ref[...], g_ref[...]).astype(BF16)
    cos = cos_ref[...]
    sin = sin_ref[...]
    ones = _group_ones()

    za = jnp.dot(h, wa_ref[...], preferred_element_type=F32)
    zb = jnp.dot(h, wb_ref[...], preferred_element_type=F32)
    for c in range(4):
        sl = slice(c * LANES, (c + 1) * LANES)
        oa_ref[:, sl] = _rope64(_head64_norm(za[:, sl], gqa_ref[:, sl], ones), cos, sin)
    for c in range(2):
        sl = slice(512 + c * LANES, 512 + (c + 1) * LANES)
        oa_ref[:, sl] = _rope64(_head64_norm(za[:, sl], gka_ref[...], ones), cos, sin)
    oa_ref[:, 768:1024] = za[:, 768:1024]

    zc = jnp.dot(h, wc_ref[...], preferred_element_type=F32)
    for c in range(4):
        sl = slice(c * LANES, (c + 1) * LANES)
        ob_ref[:, sl] = _head64_norm(zb[:, sl], gqb_ref[:, sl], ones)
    for c in range(4):
        sl = slice(512 + c * LANES, 512 + (c + 1) * LANES)
        ob_ref[:, sl] = _head64_norm(zb[:, sl], gkb_ref[:, c * LANES:(c + 1) * LANES], ones)
    ob_ref[:, 1024:1536] = zb[:, 1024:1536]

    zd = jnp.dot(h, wd_ref[...], preferred_element_type=F32)
    for c in range(2):
        sl = slice(c * LANES, (c + 1) * LANES)
        oc_ref[:, sl] = _rope64(zc[:, sl], cos, sin)
    for c in range(2):
        sl = slice(256 + c * LANES, 256 + (c + 1) * LANES)
        oc_ref[:, sl] = _rope64(zc[:, sl], cos, sin) * (HD ** -0.5)
    oc_ref[:, 512:1536] = zc[:, 512:1536]

    od_ref[:, 0:256] = zd[:, 0:256] * (HD ** -0.5)
    od_ref[:, 256:1536] = zd[:, 256:1536]
    pre = _dot(zd[:, 1536:1664], wup_ref[...]) + ba_ref[...]
    od_ref[:, 1536:1792] = _log_sigmoid(pre) * (LOG2E / GLA_TAU)


def _inproj(x, l, W, cos, sin, n_pos_blocks, tm):
    n, d = x.shape
    row = lambda i: (i, 0)
    lay = (l, 0, 0)
    in_specs = [
        pl.BlockSpec((tm, d), row),
        _const_spec((None, 1, d), lay),
        _const_spec((None, d, 1024), lay),
        _const_spec((None, d, 1536), lay),
        _const_spec((None, d, 1536), lay),
        _const_spec((None, d, 1664), lay),
        _const_spec((None, LANES, 256), lay),
        _const_spec((None, 1, 256), lay),
        _const_spec((None, 1, 512), lay),
        _const_spec((None, 1, LANES), lay),
        _const_spec((None, 1, 512), lay),
        _const_spec((None, 1, 512), lay),
        pl.BlockSpec((tm, LANES), lambda i: (i % n_pos_blocks, 0)),
        pl.BlockSpec((tm, LANES), lambda i: (i % n_pos_blocks, 0)),
    ]
    widths = (1024, 1536, 1536, 1792)
    return pl.pallas_call(
        _inproj_kernel,
        grid=(n // tm,),
        in_specs=in_specs,
        out_specs=[pl.BlockSpec((tm, w), row) for w in widths],
        out_shape=[jax.ShapeDtypeStruct((n, w), F32) for w in widths],
        compiler_params=_params(1, 56),
        name="inproj",
    )(x, W["norm_mix"], W["wa"], W["wb"], W["wc"], W["wd"], W["wup"], W["b_alpha"],
      W["gqa"], W["gka"], W["gqb"], W["gkb"], cos, sin)


def _stack_rows(refs):
    blocks = [r[...].astype(BF16) for r in refs]
    return blocks[0] if len(blocks) == 1 else jnp.concatenate(blocks, axis=0)


def _first_valid_key(masked_prefix):
    prev_rows, seq_tile = masked_prefix
    return prev_rows - pl.program_id(1) * seq_tile


def _attn_a_kernel(*refs, n_kv, chunk_rows, keys_per_chunk, masked_prefix):
    q_ref = refs[0]
    k_all = _stack_rows(refs[1:1 + n_kv])
    v_all = _stack_rows(refs[1 + n_kv:1 + 2 * n_kv])
    sink_ref = refs[1 + 2 * n_kv]
    o_ref = refs[2 + 2 * n_kv]
    cr = chunk_rows
    low = _lane_iota((1, LANES)) < HD
    units = [(j, g) for j in range(q_ref.shape[0] // cr) for g in range(2)]
    keys_of = lambda j: slice(j * cr, j * cr + keys_per_chunk)
    group = lambda g: slice(g * LANES, (g + 1) * LANES)
    scores = []
    for j, g in units:
        stacked = []
        for c in (2 * g, 2 * g + 1):
            q_c = q_ref[j * cr:(j + 1) * cr, c * LANES:(c + 1) * LANES]
            stacked += [jnp.where(low, q_c, 0.0), jnp.where(low, 0.0, q_c)]
        s = _dot_nt(jnp.concatenate(stacked, axis=0), k_all[keys_of(j), group(g)])
        if masked_prefix is not None:
            key_ok = _lane_iota((1, keys_per_chunk)) >= _first_valid_key(masked_prefix) - j * cr
            s = jnp.where(key_ok, s, NEG)
        scores.append(s)
    probs = []
    for (j, g), s in zip(units, scores):
        sink = sink_ref[g][:, 0:1]
        m = jnp.maximum(jnp.max(s, axis=-1, keepdims=True), sink)
        p = jnp.exp2(s - m)
        probs.append((p.astype(BF16), jnp.sum(p, axis=-1, keepdims=True) + jnp.exp2(sink - m)))
    outs = [_dot(p, v_all[keys_of(j), group(g)]) / den for (j, g), (p, den) in zip(units, probs)]
    for (j, g), o in zip(units, outs):
        for t, c in enumerate((2 * g, 2 * g + 1)):
            o_ref[j * cr:(j + 1) * cr, c * LANES:(c + 1) * LANES] = jnp.where(
                low, o[2 * t * cr:(2 * t + 1) * cr], o[(2 * t + 1) * cr:(2 * t + 2) * cr])


def _attn_a_cols_kernel(q_ref, kp_ref, kc_ref, vp_ref, vc_ref, sink_ref, o_ref, *, n_kv,
                        masked_prefix):
    del n_kv
    prev_rows = kp_ref.shape[0]
    k_all = _stack_rows((kp_ref, kc_ref))
    v_all = _stack_rows((vp_ref, vc_ref))
    cr = CHUNK
    keys_per_chunk = prev_rows + cr
    lane_low = _lane_iota((1, LANES)) < HD
    same_half = ((lax.broadcasted_iota(jnp.int32, (LANES, LANES), 0) < HD)
                 == (_lane_iota((LANES, LANES)) < HD))
    key_row = lax.broadcasted_iota(jnp.int32, (keys_per_chunk, 2 * LANES), 0)
    units = [(j, g) for j in range(q_ref.shape[0] // cr) for g in range(2)]
    keys_of = lambda j: slice(j * cr, j * cr + keys_per_chunk)
    group = lambda g: slice(g * LANES, (g + 1) * LANES)

    scores = []
    for j, g in units:
        q_t = []
        for c in (2 * g, 2 * g + 1):
            q_c = q_ref[j * cr:(j + 1) * cr, c * LANES:(c + 1) * LANES]
            both = jnp.concatenate([q_c, q_c], axis=0).T
            q_t.append(jnp.where(same_half, both, 0.0).astype(BF16))
        s_t = jnp.dot(k_all[keys_of(j), group(g)], jnp.concatenate(q_t, axis=1),
                      preferred_element_type=F32)
        if j * cr < prev_rows:
            s_t = jnp.where(key_row >= _first_valid_key(masked_prefix) - j * cr, s_t, NEG)
        scores.append(s_t)
    probs = []
    for (j, g), s_t in zip(units, scores):
        sink = sink_ref[g][0:1, :]
        m = jnp.maximum(jnp.max(s_t, axis=0, keepdims=True), sink)
        p_t = jnp.exp2(s_t - m)
        den = jnp.sum(p_t, axis=0, keepdims=True) + jnp.exp2(sink - m)
        probs.append((p_t.astype(BF16), den))
    outs = []
    for (j, g), (p_t, den) in zip(units, probs):
        outs.append(_dot_tn(v_all[keys_of(j), group(g)], p_t) / den)
    for (j, g), o_t in zip(units, outs):
        for t, c in enumerate((2 * g, 2 * g + 1)):
            blk = o_t[:, t * LANES:(t + 1) * LANES].T
            o_ref[j * cr:(j + 1) * cr, c * LANES:(c + 1) * LANES] = jnp.where(
                lane_low, blk[0:cr], blk[cr:2 * cr])


def _attn_b_kernel(*refs, n_kv, masked_prefix):
    q_ref = refs[0]
    k_all = _stack_rows(refs[1:1 + n_kv])
    v_all = _stack_rows(refs[1 + n_kv:1 + 2 * n_kv])
    bias_ref = refs[1 + 2 * n_kv]
    o_ref = refs[2 + 2 * n_kv]
    low = _lane_iota((1, LANES)) < HD
    if masked_prefix is not None:
        key_ok = _lane_iota((1, k_all.shape[0])) >= _first_valid_key(masked_prefix)
    slab = lambda c: slice(c * LANES, (c + 1) * LANES)
    heads = [(c, half) for c in range(4) for half in range(2)]
    scores = []
    for c, half in heads:
        q_c = q_ref[:, slab(c)]
        q_h = jnp.where(low, q_c, 0.0) if half == 0 else jnp.where(low, 0.0, q_c)
        scores.append(_dot_nt(q_h, k_all[:, slab(c)]))
    probs = []
    for (c, half), s in zip(heads, scores):
        s = s + bias_ref[2 * c + half]
        if masked_prefix is not None:
            s = jnp.where(key_ok, s, NEG)
        p = jnp.exp2(s - jnp.max(s, axis=-1, keepdims=True))
        probs.append((p.astype(BF16), jnp.sum(p, axis=-1, keepdims=True)))
    outs = [_dot(p, v_all[:, slab(c)]) / den for (c, half), (p, den) in zip(heads, probs)]
    for c in range(4):
        o_ref[:, slab(c)] = jnp.where(low, outs[2 * c], outs[2 * c + 1])


def _attention(kern, q_arr, kv_arrs, extra, grid, q_rows, q_index, vmem_mib, name):
    in_specs = [pl.BlockSpec((q_rows, 512), lambda b, i: (q_index(b, i), 0))]
    args = [q_arr]
    for use_v in (False, True):
        for arr, rows, kc0, vc0, ncol, index_fn in kv_arrs:
            width = ncol * LANES
            col_block = (vc0 if use_v else kc0) // width
            in_specs.append(pl.BlockSpec(
                (rows, width),
                functools.partial(lambda b, i, f, cb: (f(b, i), cb), f=index_fn, cb=col_block)))
            args.append(arr)
    in_specs.append(_const_spec(extra.shape, (0,) * extra.ndim))
    args.append(extra)
    return pl.pallas_call(
        functools.partial(kern, n_kv=len(kv_arrs)),
        grid=grid,
        in_specs=in_specs,
        out_specs=pl.BlockSpec((q_rows, 512), lambda b, i: (q_index(b, i), 0)),
        out_shape=jax.ShapeDtypeStruct((grid[0] * grid[1] * q_rows, 512), F32),
        compiler_params=_params(2, vmem_mib),
        name=name,
    )(*args)


def _stacked_sinks(sink, chunk_rows):
    col = jnp.repeat(sink.reshape(2, 4), chunk_rows, axis=1)
    return jnp.broadcast_to(col[:, :, None], (2, 4 * chunk_rows, LANES)).astype(F32)


def _head_lane_mask(head):
    low = _lane_iota((1, LANES)) < HD
    return low if head % 2 == 0 else jnp.logical_not(low)


def _ret_kernel(z_ref, dmat_ref, qdec_ref, kdec_ref, sdec_ref, gn_ref, s0_ref,
                o_ref, sout_ref, st_ref):
    i = pl.program_id(1)

    @pl.when(i == 0)
    def _():
        st_ref[...] = s0_ref[...]

    q = z_ref[:, 0:256]
    k = z_ref[:, 256:512]
    q_in = q * qdec_ref[...]
    k_st = k * kdec_ref[...]
    heads = range(C_HEADS)
    pair = lambda h: slice((h // 2) * LANES, (h // 2 + 1) * LANES)
    v = [z_ref[:, 512 + h * LANES:512 + (h + 1) * LANES].astype(BF16) for h in heads]
    old_state = [st_ref[h] for h in heads]
    raw = [_dot_nt(q[:, pair(h)], jnp.where(_head_lane_mask(h), k[:, pair(h)], 0.0)) for h in heads]
    inter = [_dot_nt(q_in[:, pair(h)], old_state[h]) for h in heads]
    state_upd = [_dot_tn(v[h], jnp.where(_head_lane_mask(h), k_st[:, pair(h)], 0.0)) for h in heads]
    scores = [(raw[h] * dmat_ref[h]).astype(BF16) for h in heads]
    outs = [jnp.dot(scores[h], v[h], preferred_element_type=F32) + inter[h] for h in heads]
    for h in heads:
        o = outs[h]
        g_h = z_ref[:, 1024 + h * LANES:1024 + (h + 1) * LANES]
        oc = o - jnp.mean(o, axis=-1, keepdims=True)
        y = oc * lax.rsqrt(jnp.mean(oc * oc, axis=-1, keepdims=True) + EPS)
        y = y * gn_ref[:, h * LANES:(h + 1) * LANES]
        o_ref[:, h * LANES:(h + 1) * LANES] = y * (g_h * _sigmoid(g_h))
        st_ref[h] = old_state[h] * sdec_ref[:, pair(h)] + state_upd[h]

    @pl.when(i == pl.num_programs(1) - 1)
    def _():
        sout_ref[...] = st_ref[...]


def _gla_kernel(z_ref, tril_ref, lmask_ref, bd_ref, ind_ref, rep_ref, gn_ref, s0_ref,
                o_ref, sout_ref, st_ref, *, levels):
    i = pl.program_id(1)

    @pl.when(i == 0)
    def _():
        st_ref[...] = s0_ref[...]

    tc = z_ref.shape[0]
    diag = bd_ref.shape[0]
    q = z_ref[:, 0:256]
    k = z_ref[:, 256:512]
    la = z_ref[:, 1536:1792]
    cum = _dot_01_left(tril_ref[...], la)
    cum_last = cum[tc - 1:tc, :]

    off_diag = []
    for size in (s for s in levels if s > diag):
        for start in range(0, tc, size):
            m = start + size // 2 - 1
            rows = slice(start + size // 2, start + size)
            cols = slice(start, start + size // 2)
            q_r = q[rows] * jnp.exp2(cum[rows] - cum[m:m + 1])
            k_c = k[cols] * jnp.exp2(cum[m:m + 1] - cum[cols])
            off_diag.append((rows, cols, q_r, k_c))
    small = [s for s in levels if s <= diag]
    q_lv, k_lv = [], []
    for size in small:
        mids = []
        for start in range(0, tc, size):
            m = start + size // 2 - 1
            mids.append(jnp.broadcast_to(cum[m:m + 1, :], (size, 256)))
        cm = mids[0] if len(mids) == 1 else jnp.concatenate(mids, axis=0)
        e = jnp.exp2(-jnp.abs(cum - cm))
        q_lv.append(q * e)
        k_lv.append(k * e)

    nb = tc // SUB
    q3 = q.reshape(nb, SUB, 256)
    k3 = k.reshape(nb, SUB, 256)
    c3 = cum.reshape(nb, SUB, 256)
    row_in_block = lax.broadcasted_iota(jnp.int32, (1, SUB, 1), 1)
    r = None
    for j in range(SUB):
        e = jnp.exp2(jnp.where(row_in_block >= j, c3 - c3[:, j:j + 1, :], NEG))
        t = q3 * k3[:, j:j + 1, :] * e
        part = jnp.dot(t.reshape(tc, 256).astype(BF16), ind_ref[j], preferred_element_type=F32)
        r = part if r is None else r + part
    r = r.astype(BF16)
    bd = bd_ref[...]

    q_in = q * jnp.exp2(cum)
    k_st = k * jnp.exp2(cum_last - cum)
    s_dec = jnp.exp2(cum_last)

    heads = range(C_HEADS)
    pair = lambda h: slice((h // 2) * LANES, (h // 2 + 1) * LANES)
    block = lambda b: slice(b * diag, (b + 1) * diag)
    n_blocks = tc // diag
    v = [z_ref[:, 512 + h * LANES:512 + (h + 1) * LANES].astype(BF16) for h in heads]
    old_state = [st_ref[h] for h in heads]
    inter = [_dot_nt(q_in[:, pair(h)], old_state[h]) for h in heads]
    state_upd = [_dot_tn(v[h], jnp.where(_head_lane_mask(h), k_st[:, pair(h)], 0.0)) for h in heads]
    raw = {}
    for h in heads:
        hm = _head_lane_mask(h)
        for b in range(n_blocks):
            raw[h, b, "blk8"] = jnp.dot(r[block(b)], rep_ref[h], preferred_element_type=F32)
            for lv in range(len(small)):
                k_h = jnp.where(hm, k_lv[lv][block(b), pair(h)], 0.0)
                raw[h, b, lv] = _dot_nt(q_lv[lv][block(b), pair(h)], k_h)
        for n, (rows, cols, q_r, k_c) in enumerate(off_diag):
            raw[h, "off", n] = _dot_nt(q_r[:, pair(h)], jnp.where(hm, k_c[:, pair(h)], 0.0))
    scores = {}
    for h in heads:
        for b in range(n_blocks):
            s_b = raw[h, b, "blk8"] * bd
            for lv in range(len(small)):
                s_b = s_b + raw[h, b, lv] * lmask_ref[lv]
            scores[h, b] = s_b.astype(BF16)
    outs = []
    for h in heads:
        blocks = [jnp.dot(scores[h, b], v[h][block(b)], preferred_element_type=F32)
                  for b in range(n_blocks)]
        for n, (rows, cols, _, _) in enumerate(off_diag):
            o_rc = _dot(raw[h, "off", n], v[h][cols])
            for b in range(rows.start // diag, rows.stop // diag):
                lo = b * diag - rows.start
                blocks[b] = blocks[b] + o_rc[lo:lo + diag]
        o = blocks[0] if n_blocks == 1 else jnp.concatenate(blocks, axis=0)
        outs.append(o + inter[h])
    for h in heads:
        o = outs[h]
        g_h = z_ref[:, 1024 + h * LANES:1024 + (h + 1) * LANES]
        y = o * lax.rsqrt(jnp.mean(o * o, axis=-1, keepdims=True) + EPS)
        y = y * gn_ref[:, h * LANES:(h + 1) * LANES]
        o_ref[:, h * LANES:(h + 1) * LANES] = y * (g_h * _sigmoid(g_h))
        st_ref[h] = old_state[h] * s_dec[:, pair(h)] + state_upd[h]

    @pl.when(i == pl.num_programs(1) - 1)
    def _():
        sout_ref[...] = st_ref[...]


def _ret_constants(tc):
    log_gamma = np.log1p(-np.exp2(-5.0 - np.arange(C_HEADS, dtype=np.float64)))
    idx = np.arange(tc)
    diff = idx[:, None] - idx[None, :]
    dmat = np.where(diff >= 0, np.exp(log_gamma[:, None, None] * np.maximum(diff, 0)[None]), 0.0)
    per_lane = np.repeat(log_gamma, HD)
    qdec = np.exp((idx[:, None] + 1) * per_lane[None, :])
    kdec = np.exp((tc - 1 - idx[:, None]) * per_lane[None, :])
    sdec = np.exp(tc * per_lane)[None, :]
    return [jnp.asarray(a, F32) for a in (dmat, qdec, kdec, sdec)]


def _gla_levels(tc):
    levels = []
    size = tc
    while size >= 2 * SUB:
        levels.append(size)
        size //= 2
    return tuple(levels)


def _gla_diag(tc):
    return min(tc, LANES)


def _gla_constants(tc):
    diag = _gla_diag(tc)
    idx = np.arange(tc)
    tril = (idx[None, :] <= idx[:, None]).astype(np.float32)
    idx = np.arange(diag)
    i, j = idx[:, None], idx[None, :]
    lmasks = []
    for size in (s for s in _gla_levels(tc) if s <= diag):
        lmasks.append(((i // size == j // size) & (i % size >= size // 2) & (j % size < size // 2)))
    lmask = np.stack(lmasks).astype(np.float32)
    bd = (i // SUB == j // SUB).astype(np.float32)
    ind = np.zeros((SUB, 256, LANES), np.float32)
    for jj in range(SUB):
        for h in range(C_HEADS):
            ind[jj, h * HD:(h + 1) * HD, h * SUB + jj] = 1.0
    rep = np.zeros((C_HEADS, LANES, diag), np.float32)
    for h in range(C_HEADS):
        for jj in range(SUB):
            rep[h, h * SUB + jj, jj::SUB] = 1.0
    return (jnp.asarray(tril, BF16), jnp.asarray(lmask, F32), jnp.asarray(bd, F32),
            jnp.asarray(ind, BF16), jnp.asarray(rep, BF16))


def _recurrent_specs(tc, n_tiles, width):
    z_spec = pl.BlockSpec((tc, width), lambda b, i: (b * n_tiles + i, 0))
    o_spec = pl.BlockSpec((tc, 512), lambda b, i: (b * n_tiles + i, 0))
    s_spec = pl.BlockSpec((None, C_HEADS, LANES, LANES), lambda b, i: (b, 0, 0, 0))
    return z_spec, o_spec, s_spec


def _retention(zc, l, gn, s0, batch, seq, tc):
    n_tiles = seq // tc
    consts = _ret_constants(tc)
    z_spec, o_spec, s_spec = _recurrent_specs(tc, n_tiles, 1536)
    in_specs = [z_spec,
                _const_spec((C_HEADS, tc, tc), (0, 0, 0)),
                _const_spec((tc, 256), (0, 0)),
                _const_spec((tc, 256), (0, 0)),
                _const_spec((1, 256), (0, 0)),
                _const_spec((None, 1, 512), (l, 0, 0)),
                s_spec]
    return pl.pallas_call(
        _ret_kernel,
        grid=(batch, n_tiles),
        in_specs=in_specs,
        out_specs=[o_spec, s_spec],
        out_shape=[jax.ShapeDtypeStruct((batch * seq, 512), F32),
                   jax.ShapeDtypeStruct((batch, C_HEADS, LANES, LANES), F32)],
        scratch_shapes=[pltpu.VMEM((C_HEADS, LANES, LANES), F32)],
        compiler_params=_params(2, 40),
        name="retention",
    )(zc, *consts, gn, s0)


def _gla(zd, l, gn, s0, batch, seq, tc):
    n_tiles = seq // tc
    levels = _gla_levels(tc)
    tril, lmask, bd, ind, rep = _gla_constants(tc)
    z_spec, o_spec, s_spec = _recurrent_specs(tc, n_tiles, 1792)
    in_specs = [z_spec,
                _const_spec(tril.shape, (0, 0)),
                _const_spec(lmask.shape, (0, 0, 0)),
                _const_spec(bd.shape, (0, 0)),
                _const_spec(ind.shape, (0, 0, 0)),
                _const_spec(rep.shape, (0, 0, 0)),
                _const_spec((None, 1, 512), (l, 0, 0)),
                s_spec]
    return pl.pallas_call(
        functools.partial(_gla_kernel, levels=levels),
        grid=(batch, n_tiles),
        in_specs=in_specs,
        out_specs=[o_spec, s_spec],
        out_shape=[jax.ShapeDtypeStruct((batch * seq, 512), F32),
                   jax.ShapeDtypeStruct((batch, C_HEADS, LANES, LANES), F32)],
        scratch_shapes=[pltpu.VMEM((C_HEADS, LANES, LANES), F32)],
        compiler_params=_params(2, 48),
        name="gla",
    )(zd, tril, lmask, bd, ind, rep, gn, s0)


def _state_to_kernel(s):
    st = jnp.swapaxes(s, -1, -2)
    lo = jnp.pad(st, ((0, 0), (0, 0), (0, 0), (0, HD)))
    hi = jnp.pad(st, ((0, 0), (0, 0), (0, 0), (HD, 0)))
    odd = (jnp.arange(C_HEADS) % 2 == 1)[None, :, None, None]
    return jnp.where(odd, hi, lo)


def _state_from_kernel(st):
    lo = st[..., :HD]
    hi = st[..., HD:]
    odd = (jnp.arange(C_HEADS) % 2 == 1)[None, :, None, None]
    return jnp.swapaxes(jnp.where(odd, hi, lo), -1, -2)


def _merge_kernel(x_ref, g_ref, oa_ref, ob_ref, oc_ref, od_ref, wm_ref, wbr_ref, wo_ref, y_ref):
    x = x_ref[...]
    d = x.shape[1]
    h = _rms_rows(x, g_ref[...]).astype(BF16)
    acc = None
    for b, o_ref in enumerate((oa_ref, ob_ref, oc_ref, od_ref)):
        gate = _sigmoid(jnp.dot(h, wm_ref[:, b * d:(b + 1) * d], preferred_element_type=F32))
        term = gate * _dot(o_ref[...], wbr_ref[b])
        acc = term if acc is None else acc + term
    y_ref[...] = x + _dot(acc, wo_ref[...])


def _merge(x, outs, l, W, tm):
    n, d = x.shape
    row = lambda i: (i, 0)
    lay = (l, 0, 0)
    in_specs = ([pl.BlockSpec((tm, d), row), _const_spec((None, 1, d), lay)]
                + [pl.BlockSpec((tm, 512), row)] * 4
                + [_const_spec((None, d, 4 * d), lay),
                   _const_spec((None, 4, 512, d), (l, 0, 0, 0)),
                   _const_spec((None, d, d), lay)])
    return pl.pallas_call(
        _merge_kernel,
        grid=(n // tm,),
        in_specs=in_specs,
        out_specs=pl.BlockSpec((tm, d), row),
        out_shape=jax.ShapeDtypeStruct((n, d), F32),
        compiler_params=_params(1, 48),
        name="merge",
    )(x, W["norm_mix"], *outs, W["w_merge"], W["w_branch"], W["w_out"])


def _cross_ffn_kernel(x_ref, gx_ref, wq_ref, qn_ref, mk_ref, mv_ref, wxo_ref, gf_ref, wup_ref,
                      wdn_ref, y_ref, *, n_seq):
    x = x_ref[...]
    d = x.shape[1]
    rows = x.shape[0] // n_seq
    mlen = mk_ref.shape[0] // n_seq
    hx = _rms_rows(x, gx_ref[...])
    q = _dot(hx, wq_ref[...])
    slab = lambda h: slice(h * LANES, (h + 1) * LANES)
    units = [(h, b) for h in range(4) for b in range(n_seq)]
    mem_of = lambda b: slice(b * mlen, (b + 1) * mlen)
    q_n = [(_rms_rows(q[:, slab(h)], qn_ref[...]) * (LANES ** -0.5 * LOG2E)).astype(BF16)
           for h in range(4)]
    scores = [_dot_nt(q_n[h][b * rows:(b + 1) * rows], mk_ref[mem_of(b), slab(h)])
              for h, b in units]
    probs = []
    for s in scores:
        p = jnp.exp2(s - jnp.max(s, axis=-1, keepdims=True))
        probs.append((p.astype(BF16), jnp.sum(p, axis=-1, keepdims=True)))
    outs = [_dot(p, mv_ref[mem_of(b), slab(h)]) / den for (h, b), (p, den) in zip(units, probs)]
    heads = []
    for h in range(4):
        per_seq = outs[h * n_seq:(h + 1) * n_seq]
        heads.append(per_seq[0] if n_seq == 1 else jnp.concatenate(per_seq, axis=0))
    x = x + _dot(jnp.concatenate(heads, axis=1), wxo_ref[...])
    hf = _rms_rows(x, gf_ref[...]).astype(BF16)
    acc = None
    for j in range(wup_ref.shape[1] // d):
        sl = slice(j * d, (j + 1) * d)
        u = jnp.maximum(jnp.dot(hf, wup_ref[:, sl], preferred_element_type=F32), 0.0)
        term = _dot(u * u, wdn_ref[sl, :])
        acc = term if acc is None else acc + term
    y_ref[...] = x + acc


def _cross_ffn(x, mk, mv, l, W, grid, rows, x_index, n_seq):
    n, d = x.shape
    mem_rows = mk.shape[0] // grid[0]
    lay = (l, 0, 0)
    dff = W["w_up"].shape[-1]
    in_specs = [pl.BlockSpec((rows, d), lambda b, i: (x_index(b, i), 0)),
                _const_spec((None, 1, d), lay),
                _const_spec((None, d, 512), lay),
                _const_spec((None, 1, LANES), lay),
                pl.BlockSpec((mem_rows, 512), lambda b, i: (b, 0)),
                pl.BlockSpec((mem_rows, 512), lambda b, i: (b, 0)),
                _const_spec((None, 512, d), lay),
                _const_spec((None, 1, d), lay),
                _const_spec((None, d, dff), lay),
                _const_spec((None, dff, d), lay)]
    return pl.pallas_call(
        functools.partial(_cross_ffn_kernel, n_seq=n_seq),
        grid=grid,
        in_specs=in_specs,
        out_specs=pl.BlockSpec((rows, d), lambda b, i: (x_index(b, i), 0)),
        out_shape=jax.ShapeDtypeStruct((n, d), F32),
        compiler_params=_params(2, 48),
        name="cross_ffn",
    )(x, W["norm_x"], W["w_xq"], W["qn_x"], mk, mv, W["w_xo"], W["norm_ffn"], W["w_up"],
      W["w_down"])


def _memkv_kernel(mem_ref, g_ref, w_ref, kn_ref, k_ref, v_ref):
    hm = _rms_rows(mem_ref[...], g_ref[...])
    kv = _dot(hm, w_ref[...])
    for h in range(4):
        sl = slice(h * LANES, (h + 1) * LANES)
        k_ref[:, sl] = _rms_rows(kv[:, sl], kn_ref[...])
    v_ref[...] = kv[:, 512:1024]


def _memkv(mem, W, depth):
    n, d = mem.shape
    mlen = 256
    batch = n // mlen
    lay = lambda l, b: (l, 0, 0)
    return pl.pallas_call(
        _memkv_kernel,
        grid=(depth, batch),
        in_specs=[pl.BlockSpec((mlen, d), lambda l, b: (b, 0)),
                  pl.BlockSpec((None, 1, d), lay),
                  pl.BlockSpec((None, d, 1024), lay),
                  pl.BlockSpec((None, 1, LANES), lay)],
        out_specs=[pl.BlockSpec((None, mlen, 512), lambda l, b: (l, b, 0))] * 2,
        out_shape=[jax.ShapeDtypeStruct((depth, n, 512), F32)] * 2,
        compiler_params=_params(2, 32),
        name="memkv",
    )(mem, W["norm_mem"], W["w_xkv"], W["kn_x"])


def _rope_tables(pos):
    half = HD // 2
    freqs = ROPE_THETA ** (-jnp.arange(half, dtype=F32) / half)
    ang = pos.astype(F32)[:, None] * freqs[None, :]
    cos = jnp.tile(jnp.cos(ang), (1, LANES // half))
    sin = jnp.sin(ang)
    sin = jnp.tile(jnp.concatenate([-sin, sin], axis=1), (1, LANES // HD))
    return cos, sin


def _dup_heads(w, col0, n_heads):
    lead = w.shape[:-1]
    blk = w[..., col0:col0 + n_heads * HD].reshape(*lead, n_heads, 1, HD)
    return jnp.broadcast_to(blk, (*lead, n_heads, 2, HD)).reshape(*lead, n_heads * 2 * HD)


def _prepare_weights(P):
    depth, d, _ = P["w_in"].shape
    w_in = P["w_in"]
    vec = lambda a: a.reshape(depth, 1, -1).astype(F32)
    W = {}
    W["wa"] = jnp.concatenate([w_in[..., 0:512], _dup_heads(w_in, 512, 2), _dup_heads(w_in, 640, 2)],
                              axis=-1).astype(BF16)
    W["wb"] = w_in[..., 768:2304].astype(BF16)
    W["wc"] = w_in[..., 2304:3840].astype(BF16)
    W["wd"] = jnp.pad(w_in[..., 3840:5392], ((0, 0), (0, 0), (0, LANES - GLA_RANK))).astype(BF16)
    W["wup"] = jnp.pad(P["w_alpha_up"], ((0, 0), (0, LANES - GLA_RANK), (0, 0))).astype(BF16)
    W["b_alpha"] = vec(P["b_alpha"])
    W["norm_mix"] = vec(P["norm_mix"])
    W["gqa"] = vec(jnp.tile(P["qn_a"], (1, 8)) * (HD ** -0.5 * LOG2E))
    W["gka"] = vec(jnp.tile(P["kn_a"], (1, 2)))
    W["gqb"] = vec(jnp.tile(P["qn_b"], (1, 8)) * (HD ** -0.5 * LOG2E))
    W["gkb"] = vec(jnp.tile(P["kn_b"], (1, 8)))
    W["gn_c"] = vec(P["gn_c"])
    W["gn_d"] = vec(P["gn_d"])
    W["sink"] = P["sink_a"].astype(F32) * LOG2E
    for name in ("w_merge", "w_branch", "w_out", "w_xq", "w_xkv", "w_xo", "w_up", "w_down"):
        W[name] = P[name].astype(BF16)
    for name in ("norm_x", "norm_mem", "qn_x", "kn_x", "norm_ffn"):
        W[name] = vec(P[name])
    return W


def _band_mask(q_rows, prev_rows, n_prev_chunks):
    r = np.arange(q_rows)[:, None] // CHUNK
    u = np.arange(prev_rows + q_rows)[None, :] // CHUNK - prev_rows // CHUNK
    return (u <= r) & (u >= r - n_prev_chunks)


def _rel_bias(table, q_rows, prev_rows):
    nk = prev_rows + q_rows
    period = q_rows + nk
    j = np.arange(period)
    offset = np.where(j < nk, j, j - period)
    idx = np.clip(prev_rows - offset, -REL_CLIP, REL_CLIP) + REL_CLIP
    v = table[:, idx]
    skew = jnp.tile(v, (1, q_rows))[:, :q_rows * (period - 1)]
    return skew.reshape(table.shape[0], q_rows, period - 1)[:, :, :nk]


def _mixers_prompt(slabs, l, W, relbias, batch, seq):
    za, zb, zc, zd = slabs
    tq = SEQ_TILE
    n_tiles = seq // tq
    cur = lambda b, i: b * n_tiles + i

    def prev(rows, back):
        per_tile = tq // rows
        return lambda b, i: jnp.maximum(b * n_tiles * per_tile + i * per_tile - back,
                                        b * n_tiles * per_tile)

    kern_a = functools.partial(_attn_a_cols_kernel, masked_prefix=(A_PREV, tq))
    sink_rows = jnp.broadcast_to(jnp.repeat(W["sink"][l].reshape(2, 1, 4), CHUNK, axis=2),
                                 (2, SUB, 4 * CHUNK))
    oa = _attention(
        kern_a, za,
        [(za, A_PREV, 512, 768, 2, prev(A_PREV, 1)), (za, tq, 512, 768, 2, cur)],
        sink_rows, (batch, n_tiles), tq, cur, 32, "attn_a")

    bias_b = _rel_bias(relbias, tq, B_PREV) * LOG2E + jnp.asarray(
        np.where(_band_mask(tq, B_PREV, 8), 0.0, NEG)[None], F32)
    n_prev_b = B_PREV // tq
    kv_b = [(zb, tq, 512, 1024, 4, prev(tq, n_prev_b - j)) for j in range(n_prev_b)]
    kv_b.append((zb, tq, 512, 1024, 4, cur))
    kern_b = functools.partial(_attn_b_kernel, masked_prefix=(B_PREV, tq))
    ob = _attention(kern_b, zb, kv_b, bias_b, (batch, n_tiles), tq, cur, 56, "attn_b")

    zero_state = jnp.zeros((batch, C_HEADS, LANES, LANES), F32)
    oc, sc = _retention(zc, l, W["gn_c"], zero_state, batch, seq, tq)
    od, sd = _gla(zd, l, W["gn_d"], zero_state, batch, seq, tq)
    return (oa, ob, oc, od), (sc, sd)


def _mixers_sample(slabs, l, W, relbias, cache, batch, seq):
    za, zb, zc, zd = slabs
    win_k, win_v, band_k, band_v, s_c0, s_d0 = cache
    cur = lambda b, i: b
    la = win_k.shape[1]
    lb = band_k.shape[1]
    ck = jnp.broadcast_to(win_k[:, :, :, None, :], (batch, la, 2, 2, HD)).reshape(batch * la, 256)
    cv = jnp.broadcast_to(win_v[:, :, :, None, :], (batch, la, 2, 2, HD)).reshape(batch * la, 256)
    cache_a = jnp.concatenate([ck, cv], axis=1)
    kern_a = functools.partial(_attn_a_kernel, chunk_rows=seq, keys_per_chunk=la + seq,
                               masked_prefix=None)
    oa = _attention(
        kern_a, za,
        [(cache_a, la, 0, 256, 2, cur), (za, seq, 512, 768, 2, cur)],
        _stacked_sinks(W["sink"][l], seq), (batch, 1), seq, cur, 32, "attn_a_step")

    cache_b = jnp.concatenate([band_k.reshape(batch * lb, 512), band_v.reshape(batch * lb, 512)],
                              axis=1)
    bias_b = _rel_bias(relbias, seq, lb) * LOG2E
    kern_b = functools.partial(_attn_b_kernel, masked_prefix=None)
    ob = _attention(
        kern_b, zb,
        [(cache_b, lb, 0, 512, 4, cur), (zb, seq, 512, 1024, 4, cur)],
        bias_b, (batch, 1), seq, cur, 32, "attn_b_step")

    oc, sc = _retention(zc, l, W["gn_c"], _state_to_kernel(s_c0), batch, seq, seq)
    od, sd = _gla(zd, l, W["gn_d"], _state_to_kernel(s_d0), batch, seq, seq)
    return (oa, ob, oc, od), (sc, sd)


def _new_rows(slabs, batch, seq, rows_a, rows_b):
    za = slabs[0].reshape(batch, seq, -1)[:, seq - rows_a:]
    zb = slabs[1].reshape(batch, seq, -1)[:, seq - rows_b:]
    ka = za[:, :, 512:768].reshape(batch, rows_a, 2, 2, HD)[:, :, :, 0, :]
    va = za[:, :, 768:1024].reshape(batch, rows_a, 2, 2, HD)[:, :, :, 0, :]
    kb = zb[:, :, 512:1024].reshape(batch, rows_b, 8, HD)
    vb = zb[:, :, 1024:1536].reshape(batch, rows_b, 8, HD)
    return ka, va, kb, vb


def kernel(x_prompt, x_sample, cache_win_k, cache_win_v, cache_band_k, cache_band_v, state_ret, state_gla, cache_mem_k, cache_mem_v, mem_prompt, norm_mix, w_in, qn_a, kn_a, sink_a, qn_b, kn_b, relbias_b, gn_c, w_alpha_up, b_alpha, gn_d, w_branch, w_merge, w_out, norm_x, norm_mem, w_xq, w_xkv, qn_x, kn_x, w_xo, norm_ffn, w_up, w_down):
    P = dict(norm_mix=norm_mix, w_in=w_in, qn_a=qn_a, kn_a=kn_a, sink_a=sink_a, qn_b=qn_b,
             kn_b=kn_b, relbias_b=relbias_b, gn_c=gn_c, w_alpha_up=w_alpha_up, b_alpha=b_alpha,
             gn_d=gn_d, w_branch=w_branch, w_merge=w_merge, w_out=w_out, norm_x=norm_x,
             norm_mem=norm_mem, w_xq=w_xq, w_xkv=w_xkv, qn_x=qn_x, kn_x=kn_x, w_xo=w_xo,
             norm_ffn=norm_ffn, w_up=w_up, w_down=w_down)
    depth = w_in.shape[0]
    bp, tp, d = x_prompt.shape
    bs, ts, _ = x_sample.shape
    mlen = mem_prompt.shape[1]
    W = _prepare_weights(P)

    cos_p, sin_p = _rope_tables(jnp.arange(tp))
    cos_s, sin_s = _rope_tables(jnp.tile(PAST_LEN + jnp.arange(ts), bs))
    mem_k, mem_v = _memkv(mem_prompt.reshape(bp * mlen, d), W, depth)

    xp = x_prompt.reshape(bp * tp, d)
    xs = x_sample.reshape(bs * ts, d)
    tm = SEQ_TILE
    n_tiles = tp // tm
    rows_p, rows_s = [], []
    st_p, st_s = [], []
    for l in range(depth):
        relbias = relbias_b[l].astype(F32)
        slabs = _inproj(xp, l, W, cos_p, sin_p, tp // tm, tm)
        outs, states = _mixers_prompt(slabs, l, W, relbias, bp, tp)
        rows_p.append(_new_rows(slabs, bp, tp, min(A_PREV, tp), min(B_PREV, tp)))
        st_p.append(states)
        xp = _merge(xp, outs, l, W, tm)
        xp = _cross_ffn(xp, mem_k[l], mem_v[l], l, W, (bp, n_tiles), tm,
                        lambda b, i: b * n_tiles + i, 1)
        slabs = _inproj(xs, l, W, cos_s, sin_s, 1, bs * ts)
        cache = (cache_win_k[l], cache_win_v[l], cache_band_k[l], cache_band_v[l],
                 state_ret[l], state_gla[l])
        outs, states = _mixers_sample(slabs, l, W, relbias, cache, bs, ts)
        rows_s.append(_new_rows(slabs, bs, ts, ts, ts))
        st_s.append(states)
        xs = _merge(xs, outs, l, W, bs * ts)
        xs = _cross_ffn(xs, cache_mem_k[l].reshape(bs * mlen, 512),
                        cache_mem_v[l].reshape(bs * mlen, 512), l, W, (1, 1), bs * ts,
                        lambda b, i: 0, bs)

    def stack_rows(rows, k):
        return jnp.stack([r[k] for r in rows])

    def stack_state(states, k):
        return jnp.stack([_state_from_kernel(s[k]) for s in states])

    return (xp.reshape(bp, tp, d), xs.reshape(bs, ts, d),
            stack_rows(rows_p, 0), stack_rows(rows_p, 1), stack_rows(rows_p, 2), stack_rows(rows_p, 3),
            stack_state(st_p, 0), stack_state(st_p, 1),
            mem_k.reshape(depth, bp, mlen, 4, LANES), mem_v.reshape(depth, bp, mlen, 4, LANES),
            stack_rows(rows_s, 0), stack_rows(rows_s, 1), stack_rows(rows_s, 2), stack_rows(rows_s, 3),
            stack_state(st_s, 0), stack_state(st_s, 1))
```

```python
import functools

import numpy as np
import jax
import jax.numpy as jnp
from jax import lax
from jax.experimental import pallas as pl
from jax.experimental.pallas import tpu as pltpu

F32 = jnp.float32
BF16 = jnp.bfloat16

EPS = 1e-6
NEG = -1e30
LOG2E = 1.4426950408889634
PAST_LEN = 1024
CHUNK = 64
ROPE_THETA = 10000.0
A_PREV = 2 * CHUNK
B_PREV = 8 * CHUNK
REL_CLIP = 128
GLA_RANK = 16
GLA_TAU = 16.0
C_HEADS = 4
LANES = 128
SUB = 8
HD = 64
SEQ_TILE = 256
TOKEN_TILE = 512
MIB = 2 ** 20


def _dot(a, b):
    return jnp.dot(a.astype(BF16), b.astype(BF16), preferred_element_type=F32)


def _dot_nt(a, b):
    return lax.dot_general(a.astype(BF16), b.astype(BF16), (((1,), (1,)), ((), ())),
                           preferred_element_type=F32)


def _dot_tn(a, b):
    return lax.dot_general(a.astype(BF16), b.astype(BF16), (((0,), (0,)), ((), ())),
                           preferred_element_type=F32)


def _split3(x):
    x1 = x.astype(BF16)
    r1 = x - x1.astype(F32)
    x2 = r1.astype(BF16)
    x3 = (r1 - x2.astype(F32)).astype(BF16)
    return x1, x2, x3


def _dot_01_left(m01, x):
    x1, x2, x3 = _split3(x)
    dot = functools.partial(jnp.dot, preferred_element_type=F32)
    return dot(m01, x1) + dot(m01, x2) + dot(m01, x3)


def _rms_rows(x, g):
    return x * lax.rsqrt(jnp.mean(x * x, axis=-1, keepdims=True) + EPS) * g


def _lane_iota(shape):
    return lax.broadcasted_iota(jnp.int32, shape, len(shape) - 1)


def _group_ones(width):
    r = lax.broadcasted_iota(jnp.int32, (width, width), 0) // HD
    c = lax.broadcasted_iota(jnp.int32, (width, width), 1) // HD
    return jnp.where(r == c, 1.0, 0.0).astype(BF16)


def _head64_norm(x, g, ones):
    sq = x * x
    hi = sq.astype(BF16)
    lo = (sq - hi.astype(F32)).astype(BF16)
    ss = (jnp.dot(hi, ones, preferred_element_type=F32)
          + jnp.dot(lo, ones, preferred_element_type=F32))
    return x * lax.rsqrt(ss * (1.0 / HD) + EPS) * g


def _rope64(x, cos, sin):
    first_half = (_lane_iota((1, LANES)) % HD) < (HD // 2)
    swapped = jnp.where(first_half, pltpu.roll(x, LANES - HD // 2, 1), pltpu.roll(x, HD // 2, 1))
    return x * cos + swapped * sin


def _sigmoid(x):
    return 1.0 / (1.0 + jnp.exp(-x))


def _log_sigmoid(x):
    return jnp.minimum(x, 0.0) - jnp.log(1.0 + jnp.exp(-jnp.abs(x)))


def _params(n_axes, vmem_mib):
    return pltpu.CompilerParams(dimension_semantics=("arbitrary",) * n_axes,
                                vmem_limit_bytes=vmem_mib * MIB)


def _const_spec(shape, index):
    return pl.BlockSpec(shape, lambda *_: index, pipeline_mode=pl.Buffered(1))


def _inproj_kernel(x_ref, g_ref, wa_ref, wb_ref, wc_ref, wd_ref, wup_ref, ba_ref,
                   gqa_ref, gka_ref, gqb_ref, gkb_ref, cos_ref, sin_ref,
                   oa_ref, ob_ref, oc_ref, od_ref):
    h = _rms_rows(x_ref[...], g_ref[...]).astype(BF16)
    cos = cos_ref[...]
    sin = sin_ref[...]
    pair = 2 * LANES
    ones = _group_ones(pair)

    za = jnp.dot(h, wa_ref[...], preferred_element_type=F32)
    zb = jnp.dot(h, wb_ref[...], preferred_element_type=F32)
    for p in range(3):
        sl = slice(p * pair, (p + 1) * pair)
        gain = gqa_ref[:, sl] if p < 2 else gka_ref[...]
        y = _head64_norm(za[:, sl], gain, ones)
        for c in range(2):
            oa_ref[:, p * pair + c * LANES:p * pair + (c + 1) * LANES] = _rope64(
                y[:, c * LANES:(c + 1) * LANES], cos, sin)
    oa_ref[:, 768:1024] = za[:, 768:1024]

    zc = jnp.dot(h, wc_ref[...], preferred_element_type=F32)
    for p in range(4):
        sl = slice(p * pair, (p + 1) * pair)
        gain = gqb_ref[:, sl] if p < 2 else gkb_ref[:, (p - 2) * pair:(p - 1) * pair]
        ob_ref[:, sl] = _head64_norm(zb[:, sl], gain, ones)
    ob_ref[:, 1024:1536] = zb[:, 1024:1536]

    zd = jnp.dot(h, wd_ref[...], preferred_element_type=F32)
    for c in range(2):
        sl = slice(c * LANES, (c + 1) * LANES)
        oc_ref[:, sl] = _rope64(zc[:, sl], cos, sin)
    for c in range(2):
        sl = slice(256 + c * LANES, 256 + (c + 1) * LANES)
        oc_ref[:, sl] = _rope64(zc[:, sl], cos, sin) * (HD ** -0.5)
    oc_ref[:, 512:1536] = zc[:, 512:1536]

    od_ref[:, 0:256] = zd[:, 0:256] * (HD ** -0.5)
    od_ref[:, 256:1536] = zd[:, 256:1536]
    pre = _dot(zd[:, 1536:1664], wup_ref[...]) + ba_ref[...]
    od_ref[:, 1536:1792] = _log_sigmoid(pre) * (LOG2E / GLA_TAU)


def _inproj(x, l, W, cos, sin, n_pos_blocks, tm):
    n, d = x.shape
    row = lambda i: (i, 0)
    lay = (l, 0, 0)
    in_specs = [
        pl.BlockSpec((tm, d), row),
        _const_spec((None, 1, d), lay),
        _const_spec((None, d, 1024), lay),
        _const_spec((None, d, 1536), lay),
        _const_spec((None, d, 1536), lay),
        _const_spec((None, d, 1664), lay),
        _const_spec((None, LANES, 256), lay),
        _const_spec((None, 1, 256), lay),
        _const_spec((None, 1, 512), lay),
        _const_spec((None, 1, 256), lay),
        _const_spec((None, 1, 512), lay),
        _const_spec((None, 1, 512), lay),
        pl.BlockSpec((tm, LANES), lambda i: (i % n_pos_blocks, 0)),
        pl.BlockSpec((tm, LANES), lambda i: (i % n_pos_blocks, 0)),
    ]
    widths = (1024, 1536, 1536, 1792)
    return pl.pallas_call(
        _inproj_kernel,
        grid=(n // tm,),
        in_specs=in_specs,
        out_specs=[pl.BlockSpec((tm, w), row) for w in widths],
        out_shape=[jax.ShapeDtypeStruct((n, w), F32) for w in widths],
        compiler_params=_params(1, 56),
        name="inproj",
    )(x, W["norm_mix"], W["wa"], W["wb"], W["wc"], W["wd"], W["wup"], W["b_alpha"],
      W["gqa"], W["gka"], W["gqb"], W["gkb"], cos, sin)


def _stack_rows(refs):
    blocks = [r[...].astype(BF16) for r in refs]
    return blocks[0] if len(blocks) == 1 else jnp.concatenate(blocks, axis=0)


def _first_valid_key(masked_prefix):
    prev_rows, seq_tile = masked_prefix
    return prev_rows - pl.program_id(1) * seq_tile


def _attn_a_kernel(*refs, n_kv, chunk_rows, keys_per_chunk, masked_prefix):
    q_ref = refs[0]
    k_all = _stack_rows(refs[1:1 + n_kv])
    v_all = _stack_rows(refs[1 + n_kv:1 + 2 * n_kv])
    sink_ref = refs[1 + 2 * n_kv]
    o_ref = refs[2 + 2 * n_kv]
    cr = chunk_rows
    low = _lane_iota((1, LANES)) < HD
    units = [(j, g) for j in range(q_ref.shape[0] // cr) for g in range(2)]
    keys_of = lambda j: slice(j * cr, j * cr + keys_per_chunk)
    group = lambda g: slice(g * LANES, (g + 1) * LANES)
    scores = []
    for j, g in units:
        stacked = []
        for c in (2 * g, 2 * g + 1):
            q_c = q_ref[j * cr:(j + 1) * cr, c * LANES:(c + 1) * LANES]
            stacked += [jnp.where(low, q_c, 0.0), jnp.where(low, 0.0, q_c)]
        s = _dot_nt(jnp.concatenate(stacked, axis=0), k_all[keys_of(j), group(g)])
        if masked_prefix is not None:
            key_ok = _lane_iota((1, keys_per_chunk)) >= _first_valid_key(masked_prefix) - j * cr
            s = jnp.where(key_ok, s, NEG)
        scores.append(s)
    probs = []
    for (j, g), s in zip(units, scores):
        sink = sink_ref[g][:, 0:1]
        m = jnp.maximum(jnp.max(s, axis=-1, keepdims=True), sink)
        p = jnp.exp2(s - m)
        probs.append((p.astype(BF16), jnp.sum(p, axis=-1, keepdims=True) + jnp.exp2(sink - m)))
    outs = [_dot(p, v_all[keys_of(j), group(g)]) / den for (j, g), (p, den) in zip(units, probs)]
    for (j, g), o in zip(units, outs):
        for t, c in enumerate((2 * g, 2 * g + 1)):
            o_ref[j * cr:(j + 1) * cr, c * LANES:(c + 1) * LANES] = jnp.where(
                low, o[2 * t * cr:(2 * t + 1) * cr],
                o[(2 * t + 1) * cr:(2 * t + 2) * cr]).astype(o_ref.dtype)


def _attn_a_cols_kernel(q_ref, kp_ref, kc_ref, vp_ref, vc_ref, sink_ref, o_ref, *, masked_prefix):
    prev_rows = kp_ref.shape[0]
    k_all = _stack_rows((kp_ref, kc_ref))
    v_all = _stack_rows((vp_ref, vc_ref))
    cr = CHUNK
    keys_per_chunk = prev_rows + cr
    lane_low = _lane_iota((1, LANES)) < HD
    same_half = ((lax.broadcasted_iota(jnp.int32, (LANES, LANES), 0) < HD)
                 == (_lane_iota((LANES, LANES)) < HD))
    key_row = lax.broadcasted_iota(jnp.int32, (keys_per_chunk, 2 * LANES), 0)
    units = [(j, g) for j in range(q_ref.shape[0] // cr) for g in range(2)]
    keys_of = lambda j: slice(j * cr, j * cr + keys_per_chunk)
    group = lambda g: slice(g * LANES, (g + 1) * LANES)

    scores = []
    for j, g in units:
        q_t = []
        for c in (2 * g, 2 * g + 1):
            q_c = q_ref[j * cr:(j + 1) * cr, c * LANES:(c + 1) * LANES]
            both = jnp.concatenate([q_c, q_c], axis=0).T
            q_t.append(jnp.where(same_half, both, 0.0).astype(BF16))
        s_t = jnp.dot(k_all[keys_of(j), group(g)], jnp.concatenate(q_t, axis=1),
                      preferred_element_type=F32)
        if j * cr < prev_rows:
            s_t = jnp.where(key_row >= _first_valid_key(masked_prefix) - j * cr, s_t, NEG)
        scores.append(s_t)
    probs = []
    for (j, g), s_t in zip(units, scores):
        sink = sink_ref[g][0:1, :]
        m = jnp.maximum(jnp.max(s_t, axis=0, keepdims=True), sink)
        p_t = jnp.exp2(s_t - m)
        den = jnp.sum(p_t, axis=0, keepdims=True) + jnp.exp2(sink - m)
        probs.append((p_t.astype(BF16), den))
    outs = []
    for (j, g), (p_t, den) in zip(units, probs):
        outs.append(_dot_tn(v_all[keys_of(j), group(g)], p_t) / den)
    for (j, g), o_t in zip(units, outs):
        for t, c in enumerate((2 * g, 2 * g + 1)):
            blk = o_t[:, t * LANES:(t + 1) * LANES].T
            o_ref[j * cr:(j + 1) * cr, c * LANES:(c + 1) * LANES] = jnp.where(
                lane_low, blk[0:cr], blk[cr:2 * cr]).astype(o_ref.dtype)


def _attn_b_kernel(*refs, n_kv, masked_prefix):
    q_ref = refs[0]
    k_all = _stack_rows(refs[1:1 + n_kv])
    v_all = _stack_rows(refs[1 + n_kv:1 + 2 * n_kv])
    bias_ref = refs[1 + 2 * n_kv]
    o_ref = refs[2 + 2 * n_kv]
    low = _lane_iota((1, LANES)) < HD
    if masked_prefix is not None:
        key_ok = _lane_iota((1, k_all.shape[0])) >= _first_valid_key(masked_prefix)
    slab = lambda c: slice(c * LANES, (c + 1) * LANES)
    heads = [(c, half) for c in range(4) for half in range(2)]
    scores = []
    for c, half in heads:
        q_c = q_ref[:, slab(c)]
        q_h = jnp.where(low, q_c, 0.0) if half == 0 else jnp.where(low, 0.0, q_c)
        scores.append(_dot_nt(q_h, k_all[:, slab(c)]))
    probs = []
    for (c, half), s in zip(heads, scores):
        s = s + bias_ref[2 * c + half]
        if masked_prefix is not None:
            s = jnp.where(key_ok, s, NEG)
        p = jnp.exp2(s - jnp.max(s, axis=-1, keepdims=True))
        probs.append((p.astype(BF16), jnp.sum(p, axis=-1, keepdims=True)))
    outs = [_dot(p, v_all[:, slab(c)]) / den for (c, half), (p, den) in zip(heads, probs)]
    for c in range(4):
        o_ref[:, slab(c)] = jnp.where(low, outs[2 * c], outs[2 * c + 1]).astype(o_ref.dtype)


def _attention_operands(q_arr, kv_arrs, extra, q_rows, q_index):
    in_specs = [pl.BlockSpec((q_rows, 512), lambda b, i: (q_index(b, i), 0))]
    args = [q_arr]
    for use_v in (False, True):
        for arrs, rows, kc0, vc0, ncol, index_fn in kv_arrs:
            width = ncol * LANES
            col_block = (vc0 if use_v else kc0) // width
            in_specs.append(pl.BlockSpec(
                (rows, width),
                functools.partial(lambda b, i, f, cb: (f(b, i), cb), f=index_fn, cb=col_block)))
            args.append(arrs[1] if use_v else arrs[0])
    in_specs.append(_const_spec(extra.shape, (0,) * extra.ndim))
    args.append(extra)
    return in_specs, args


def _stacked_sinks(sink, chunk_rows):
    col = jnp.repeat(sink.reshape(2, 4), chunk_rows, axis=1)
    return jnp.broadcast_to(col[:, :, None], (2, 4 * chunk_rows, LANES)).astype(F32)


def _head_lane_mask(head):
    low = _lane_iota((1, LANES)) < HD
    return low if head % 2 == 0 else jnp.logical_not(low)


def _ret_kernel(z_ref, dmat_ref, qdec_ref, kdec_ref, sdec_ref, gn_ref, s0_ref,
                o_ref, sout_ref, st_ref):
    i = pl.program_id(1)

    @pl.when(i == 0)
    def _():
        st_ref[...] = s0_ref[...]

    q = z_ref[:, 0:256]
    k = z_ref[:, 256:512]
    q_in = q * qdec_ref[...]
    k_st = k * kdec_ref[...]
    heads = range(C_HEADS)
    pair = lambda h: slice((h // 2) * LANES, (h // 2 + 1) * LANES)
    v = [z_ref[:, 512 + h * LANES:512 + (h + 1) * LANES].astype(BF16) for h in heads]
    old_state = [st_ref[h] for h in heads]
    raw = [_dot_nt(q[:, pair(h)], jnp.where(_head_lane_mask(h), k[:, pair(h)], 0.0)) for h in heads]
    inter = [_dot_nt(q_in[:, pair(h)], old_state[h]) for h in heads]
    state_upd = [_dot_tn(v[h], jnp.where(_head_lane_mask(h), k_st[:, pair(h)], 0.0)) for h in heads]
    scores = [(raw[h] * dmat_ref[h]).astype(BF16) for h in heads]
    outs = [jnp.dot(scores[h], v[h], preferred_element_type=F32) + inter[h] for h in heads]
    for h in heads:
        o = outs[h]
        g_h = z_ref[:, 1024 + h * LANES:1024 + (h + 1) * LANES]
        oc = o - jnp.mean(o, axis=-1, keepdims=True)
        y = oc * lax.rsqrt(jnp.mean(oc * oc, axis=-1, keepdims=True) + EPS)
        y = y * gn_ref[:, h * LANES:(h + 1) * LANES]
        o_ref[:, h * LANES:(h + 1) * LANES] = (y * (g_h * _sigmoid(g_h))).astype(o_ref.dtype)
        st_ref[h] = old_state[h] * sdec_ref[:, pair(h)] + state_upd[h]

    @pl.when(i == pl.num_programs(1) - 1)
    def _():
        sout_ref[...] = st_ref[...]


def _gla_kernel(z_ref, tril_ref, lmask_ref, bd_ref, ind_ref, rep_ref, gn_ref, s0_ref,
                o_ref, sout_ref, st_ref, *, levels):
    i = pl.program_id(1)

    @pl.when(i == 0)
    def _():
        st_ref[...] = s0_ref[...]

    tc = z_ref.shape[0]
    diag = bd_ref.shape[0]
    q = z_ref[:, 0:256]
    k = z_ref[:, 256:512]
    la = z_ref[:, 1536:1792]
    cum = _dot_01_left(tril_ref[...], la)
    cum_last = cum[tc - 1:tc, :]

    off_diag = []
    for size in (s for s in levels if s > diag):
        for start in range(0, tc, size):
            m = start + size // 2 - 1
            rows = slice(start + size // 2, start + size)
            cols = slice(start, start + size // 2)
            q_r = q[rows] * jnp.exp2(cum[rows] - cum[m:m + 1])
            k_c = k[cols] * jnp.exp2(cum[m:m + 1] - cum[cols])
            off_diag.append((rows, cols, q_r, k_c))
    small = [s for s in levels if s <= diag]
    q_lv, k_lv = [], []
    for size in small:
        mids = []
        for start in range(0, tc, size):
            m = start + size // 2 - 1
            mids.append(jnp.broadcast_to(cum[m:m + 1, :], (size, 256)))
        cm = mids[0] if len(mids) == 1 else jnp.concatenate(mids, axis=0)
        e = jnp.exp2(-jnp.abs(cum - cm))
        q_lv.append(q * e)
        k_lv.append(k * e)

    nb = tc // SUB
    q3 = q.reshape(nb, SUB, 256)
    k3 = k.reshape(nb, SUB, 256)
    c3 = cum.reshape(nb, SUB, 256)
    row_in_block = lax.broadcasted_iota(jnp.int32, (1, SUB, 1), 1)
    r = None
    for j in range(SUB):
        e = jnp.exp2(jnp.where(row_in_block >= j, c3 - c3[:, j:j + 1, :], NEG))
        t = q3 * k3[:, j:j + 1, :] * e
        part = jnp.dot(t.reshape(tc, 256).astype(BF16), ind_ref[j], preferred_element_type=F32)
        r = part if r is None else r + part
    r = r.astype(BF16)
    bd = bd_ref[...]

    q_in = q * jnp.exp2(cum)
    k_st = k * jnp.exp2(cum_last - cum)
    s_dec = jnp.exp2(cum_last)

    heads = range(C_HEADS)
    pair = lambda h: slice((h // 2) * LANES, (h // 2 + 1) * LANES)
    block = lambda b: slice(b * diag, (b + 1) * diag)
    n_blocks = tc // diag
    v = [z_ref[:, 512 + h * LANES:512 + (h + 1) * LANES].astype(BF16) for h in heads]
    old_state = [st_ref[h] for h in heads]
    inter = [_dot_nt(q_in[:, pair(h)], old_state[h]) for h in heads]
    state_upd = [_dot_tn(v[h], jnp.where(_head_lane_mask(h), k_st[:, pair(h)], 0.0)) for h in heads]
    raw = {}
    for h in heads:
        hm = _head_lane_mask(h)
        for b in range(n_blocks):
            raw[h, b, "blk8"] = jnp.dot(r[block(b)], rep_ref[h], preferred_element_type=F32)
            for lv in range(len(small)):
                k_h = jnp.where(hm, k_lv[lv][block(b), pair(h)], 0.0)
                raw[h, b, lv] = _dot_nt(q_lv[lv][block(b), pair(h)], k_h)
        for n, (rows, cols, q_r, k_c) in enumerate(off_diag):
            raw[h, "off", n] = _dot_nt(q_r[:, pair(h)], jnp.where(hm, k_c[:, pair(h)], 0.0))
    scores = {}
    for h in heads:
        for b in range(n_blocks):
            s_b = raw[h, b, "blk8"] * bd
            for lv in range(len(small)):
                s_b = s_b + raw[h, b, lv] * lmask_ref[lv]
            scores[h, b] = s_b.astype(BF16)
    outs = []
    for h in heads:
        blocks = [jnp.dot(scores[h, b], v[h][block(b)], preferred_element_type=F32)
                  for b in range(n_blocks)]
        for n, (rows, cols, _, _) in enumerate(off_diag):
            o_rc = _dot(raw[h, "off", n], v[h][cols])
            for b in range(rows.start // diag, rows.stop // diag):
                lo = b * diag - rows.start
                blocks[b] = blocks[b] + o_rc[lo:lo + diag]
        o = blocks[0] if n_blocks == 1 else jnp.concatenate(blocks, axis=0)
        outs.append(o + inter[h])
    for h in heads:
        o = outs[h]
        g_h = z_ref[:, 1024 + h * LANES:1024 + (h + 1) * LANES]
        y = o * lax.rsqrt(jnp.mean(o * o, axis=-1, keepdims=True) + EPS)
        y = y * gn_ref[:, h * LANES:(h + 1) * LANES]
        o_ref[:, h * LANES:(h + 1) * LANES] = (y * (g_h * _sigmoid(g_h))).astype(o_ref.dtype)
        st_ref[h] = old_state[h] * s_dec[:, pair(h)] + state_upd[h]

    @pl.when(i == pl.num_programs(1) - 1)
    def _():
        sout_ref[...] = st_ref[...]


def _ret_constants(tc):
    log_gamma = np.log1p(-np.exp2(-5.0 - np.arange(C_HEADS, dtype=np.float64)))
    idx = np.arange(tc)
    diff = idx[:, None] - idx[None, :]
    dmat = np.where(diff >= 0, np.exp(log_gamma[:, None, None] * np.maximum(diff, 0)[None]), 0.0)
    per_lane = np.repeat(log_gamma, HD)
    qdec = np.exp((idx[:, None] + 1) * per_lane[None, :])
    kdec = np.exp((tc - 1 - idx[:, None]) * per_lane[None, :])
    sdec = np.exp(tc * per_lane)[None, :]
    return [jnp.asarray(a, F32) for a in (dmat, qdec, kdec, sdec)]


def _gla_levels(tc):
    levels = []
    size = tc
    while size >= 2 * SUB:
        levels.append(size)
        size //= 2
    return tuple(levels)


def _gla_diag(tc):
    return min(tc, LANES)


def _gla_constants(tc):
    diag = _gla_diag(tc)
    idx = np.arange(tc)
    tril = (idx[None, :] <= idx[:, None]).astype(np.float32)
    idx = np.arange(diag)
    i, j = idx[:, None], idx[None, :]
    lmasks = []
    for size in (s for s in _gla_levels(tc) if s <= diag):
        lmasks.append(((i // size == j // size) & (i % size >= size // 2) & (j % size < size // 2)))
    lmask = np.stack(lmasks).astype(np.float32)
    bd = (i // SUB == j // SUB).astype(np.float32)
    ind = np.zeros((SUB, 256, LANES), np.float32)
    for jj in range(SUB):
        for h in range(C_HEADS):
            ind[jj, h * HD:(h + 1) * HD, h * SUB + jj] = 1.0
    rep = np.zeros((C_HEADS, LANES, diag), np.float32)
    for h in range(C_HEADS):
        for jj in range(SUB):
            rep[h, h * SUB + jj, jj::SUB] = 1.0
    return (jnp.asarray(tril, BF16), jnp.asarray(lmask, F32), jnp.asarray(bd, F32),
            jnp.asarray(ind, BF16), jnp.asarray(rep, BF16))


def _state_spec():
    return pl.BlockSpec((None, C_HEADS, LANES, LANES), lambda b, i: (b, 0, 0, 0))


def _recurrent_operands(z, consts, l, gn, s0, tc, n_tiles):
    in_specs = [pl.BlockSpec((tc, z.shape[1]), lambda b, i: (b * n_tiles + i, 0))]
    in_specs += [_const_spec(c.shape, (0,) * c.ndim) for c in consts]
    in_specs += [_const_spec((None, 1, 512), (l, 0, 0)), _state_spec()]
    return in_specs, [z, *consts, gn, s0]


def _mixers_kernel(*refs, bodies, n_in):
    ends = np.cumsum(n_in)
    a_in, b_in, c_in, d_in = (refs[e - n:e] for e, n in zip(ends, n_in))
    o_ref, sc_ref, sd_ref, stc_ref, std_ref = refs[ends[-1]:]
    body_a, body_b, body_c, body_d = bodies
    body_a(*a_in, o_ref.at[:, 0:512])
    body_b(*b_in, o_ref.at[:, 512:1024])
    body_c(*c_in, o_ref.at[:, 1024:1536], sc_ref, stc_ref)
    body_d(*d_in, o_ref.at[:, 1536:2048], sd_ref, std_ref)


def _mixers_call(parts, bodies, grid, rows, row_index, vmem_mib, name):
    in_specs = [s for specs, _ in parts for s in specs]
    args = [a for _, arrs in parts for a in arrs]
    n_in = tuple(len(specs) for specs, _ in parts)
    n_rows = grid[0] * grid[1] * rows
    state_shape = jax.ShapeDtypeStruct((grid[0], C_HEADS, LANES, LANES), F32)
    return pl.pallas_call(
        functools.partial(_mixers_kernel, bodies=bodies, n_in=n_in),
        grid=grid,
        in_specs=in_specs,
        out_specs=[pl.BlockSpec((rows, 2048), lambda b, i: (row_index(b, i), 0)),
                   _state_spec(), _state_spec()],
        out_shape=[jax.ShapeDtypeStruct((n_rows, 2048), BF16), state_shape, state_shape],
        scratch_shapes=[pltpu.VMEM((C_HEADS, LANES, LANES), F32)] * 2,
        compiler_params=_params(2, vmem_mib),
        name=name,
    )(*args)


def _state_to_kernel(s):
    st = jnp.swapaxes(s, -1, -2)
    lo = jnp.pad(st, ((0, 0), (0, 0), (0, 0), (0, HD)))
    hi = jnp.pad(st, ((0, 0), (0, 0), (0, 0), (HD, 0)))
    odd = (jnp.arange(C_HEADS) % 2 == 1)[None, :, None, None]
    return jnp.where(odd, hi, lo)


def _state_from_kernel(st):
    lo = st[..., :HD]
    hi = st[..., HD:]
    odd = (jnp.arange(C_HEADS) % 2 == 1)[None, :, None, None]
    return jnp.swapaxes(jnp.where(odd, hi, lo), -1, -2)


def _merge_kernel(x_ref, g_ref, o_ref, wm_ref, wbr_ref, wo_ref, y_ref):
    x = x_ref[...]
    d = x.shape[1]
    h = _rms_rows(x, g_ref[...]).astype(BF16)
    acc = None
    for b in range(4):
        gate = _sigmoid(jnp.dot(h, wm_ref[:, b * d:(b + 1) * d], preferred_element_type=F32))
        term = gate * jnp.dot(o_ref[:, b * 512:(b + 1) * 512], wbr_ref[b],
                              preferred_element_type=F32)
        acc = term if acc is None else acc + term
    y_ref[...] = x + _dot(acc, wo_ref[...])


def _merge(x, outs, l, W, tm):
    n, d = x.shape
    row = lambda i: (i, 0)
    lay = (l, 0, 0)
    in_specs = [pl.BlockSpec((tm, d), row), _const_spec((None, 1, d), lay),
                pl.BlockSpec((tm, 2048), row),
                _const_spec((None, d, 4 * d), lay),
                _const_spec((None, 4, 512, d), (l, 0, 0, 0)),
                _const_spec((None, d, d), lay)]
    return pl.pallas_call(
        _merge_kernel,
        grid=(n // tm,),
        in_specs=in_specs,
        out_specs=pl.BlockSpec((tm, d), row),
        out_shape=jax.ShapeDtypeStruct((n, d), F32),
        compiler_params=_params(1, 48),
        name="merge",
    )(x, W["norm_mix"], outs, W["w_merge"], W["w_branch"], W["w_out"])


def _cross_ffn_kernel(x_ref, gx_ref, wq_ref, qn_ref, mk_ref, mv_ref, wxo_ref, gf_ref, wup_ref,
                      wdn_ref, y_ref, *, n_seq):
    x = x_ref[...]
    d = x.shape[1]
    rows = x.shape[0] // n_seq
    mlen = mk_ref.shape[0] // n_seq
    hx = _rms_rows(x, gx_ref[...])
    q = _dot(hx, wq_ref[...])
    slab = lambda h: slice(h * LANES, (h + 1) * LANES)
    units = [(h, b) for h in range(4) for b in range(n_seq)]
    mem_of = lambda b: slice(b * mlen, (b + 1) * mlen)
    q_n = [(_rms_rows(q[:, slab(h)], qn_ref[...]) * (LANES ** -0.5 * LOG2E)).astype(BF16)
           for h in range(4)]
    scores = [_dot_nt(q_n[h][b * rows:(b + 1) * rows], mk_ref[mem_of(b), slab(h)])
              for h, b in units]
    probs = []
    for s in scores:
        p = jnp.exp2(s - jnp.max(s, axis=-1, keepdims=True))
        probs.append((p.astype(BF16), jnp.sum(p, axis=-1, keepdims=True)))
    outs = [_dot(p, mv_ref[mem_of(b), slab(h)]) / den for (h, b), (p, den) in zip(units, probs)]
    heads = []
    for h in range(4):
        per_seq = outs[h * n_seq:(h + 1) * n_seq]
        heads.append(per_seq[0] if n_seq == 1 else jnp.concatenate(per_seq, axis=0))
    x = x + _dot(jnp.concatenate(heads, axis=1), wxo_ref[...])
    hf = _rms_rows(x, gf_ref[...]).astype(BF16)
    acc = None
    for j in range(wup_ref.shape[1] // d):
        sl = slice(j * d, (j + 1) * d)
        u = jnp.maximum(jnp.dot(hf, wup_ref[:, sl], preferred_element_type=F32), 0.0)
        term = _dot(u * u, wdn_ref[sl, :])
        acc = term if acc is None else acc + term
    y_ref[...] = x + acc


def _cross_ffn(x, mk, mv, l, W, grid, rows, x_index, n_seq, mem_rows, mem_index):
    n, d = x.shape
    lay = (l, 0, 0)
    dff = W["w_up"].shape[-1]
    in_specs = [pl.BlockSpec((rows, d), lambda b, i: (x_index(b, i), 0)),
                _const_spec((None, 1, d), lay),
                _const_spec((None, d, 512), lay),
                _const_spec((None, 1, LANES), lay),
                pl.BlockSpec((mem_rows, 512), lambda b, i: (mem_index(b, i), 0)),
                pl.BlockSpec((mem_rows, 512), lambda b, i: (mem_index(b, i), 0)),
                _const_spec((None, 512, d), lay),
                _const_spec((None, 1, d), lay),
                _const_spec((None, d, dff), lay),
                _const_spec((None, dff, d), lay)]
    return pl.pallas_call(
        functools.partial(_cross_ffn_kernel, n_seq=n_seq),
        grid=grid,
        in_specs=in_specs,
        out_specs=pl.BlockSpec((rows, d), lambda b, i: (x_index(b, i), 0)),
        out_shape=jax.ShapeDtypeStruct((n, d), F32),
        compiler_params=_params(2, 48),
        name="cross_ffn",
    )(x, W["norm_x"], W["w_xq"], W["qn_x"], mk, mv, W["w_xo"], W["norm_ffn"], W["w_up"],
      W["w_down"])


def _memkv_kernel(mem_ref, g_ref, w_ref, kn_ref, k_ref, v_ref):
    hm = _rms_rows(mem_ref[...], g_ref[...])
    kv = _dot(hm, w_ref[...])
    for h in range(4):
        sl = slice(h * LANES, (h + 1) * LANES)
        k_ref[:, sl] = _rms_rows(kv[:, sl], kn_ref[...])
    v_ref[...] = kv[:, 512:1024]


def _memkv(mem, W, depth):
    n, d = mem.shape
    mlen = 256
    batch = n // mlen
    lay = lambda l, b: (l, 0, 0)
    return pl.pallas_call(
        _memkv_kernel,
        grid=(depth, batch),
        in_specs=[pl.BlockSpec((mlen, d), lambda l, b: (b, 0)),
                  pl.BlockSpec((None, 1, d), lay),
                  pl.BlockSpec((None, d, 1024), lay),
                  pl.BlockSpec((None, 1, LANES), lay)],
        out_specs=[pl.BlockSpec((None, mlen, 512), lambda l, b: (l, b, 0))] * 2,
        out_shape=[jax.ShapeDtypeStruct((depth, n, 512), F32)] * 2,
        compiler_params=_params(2, 32),
        name="memkv",
    )(mem, W["norm_mem"], W["w_xkv"], W["kn_x"])


def _rope_tables(pos):
    half = HD // 2
    freqs = ROPE_THETA ** (-jnp.arange(half, dtype=F32) / half)
    ang = pos.astype(F32)[:, None] * freqs[None, :]
    cos = jnp.tile(jnp.cos(ang), (1, LANES // half))
    sin = jnp.sin(ang)
    sin = jnp.tile(jnp.concatenate([-sin, sin], axis=1), (1, LANES // HD))
    return cos, sin


def _dup_heads(w, col0, n_heads):
    lead = w.shape[:-1]
    blk = w[..., col0:col0 + n_heads * HD].reshape(*lead, n_heads, 1, HD)
    return jnp.broadcast_to(blk, (*lead, n_heads, 2, HD)).reshape(*lead, n_heads * 2 * HD)


def _prepare_weights(P):
    depth, d, _ = P["w_in"].shape
    w_in = P["w_in"]
    vec = lambda a: a.reshape(depth, 1, -1).astype(F32)
    W = {}
    W["wa"] = jnp.concatenate([w_in[..., 0:512], _dup_heads(w_in, 512, 2), _dup_heads(w_in, 640, 2)],
                              axis=-1).astype(BF16)
    W["wb"] = w_in[..., 768:2304].astype(BF16)
    W["wc"] = w_in[..., 2304:3840].astype(BF16)
    W["wd"] = jnp.pad(w_in[..., 3840:5392], ((0, 0), (0, 0), (0, LANES - GLA_RANK))).astype(BF16)
    W["wup"] = jnp.pad(P["w_alpha_up"], ((0, 0), (0, LANES - GLA_RANK), (0, 0))).astype(BF16)
    W["b_alpha"] = vec(P["b_alpha"])
    W["norm_mix"] = vec(P["norm_mix"])
    W["gqa"] = vec(jnp.tile(P["qn_a"], (1, 8)) * (HD ** -0.5 * LOG2E))
    W["gka"] = vec(jnp.tile(P["kn_a"], (1, 4)))
    W["gqb"] = vec(jnp.tile(P["qn_b"], (1, 8)) * (HD ** -0.5 * LOG2E))
    W["gkb"] = vec(jnp.tile(P["kn_b"], (1, 8)))
    W["gn_c"] = vec(P["gn_c"])
    W["gn_d"] = vec(P["gn_d"])
    W["sink"] = P["sink_a"].astype(F32) * LOG2E
    for name in ("w_merge", "w_branch", "w_out", "w_xq", "w_xkv", "w_xo", "w_up", "w_down"):
        W[name] = P[name].astype(BF16)
    for name in ("norm_x", "norm_mem", "qn_x", "kn_x", "norm_ffn"):
        W[name] = vec(P[name])
    return W


def _band_mask(q_rows, prev_rows, n_prev_chunks):
    r = np.arange(q_rows)[:, None] // CHUNK
    u = np.arange(prev_rows + q_rows)[None, :] // CHUNK - prev_rows // CHUNK
    return (u <= r) & (u >= r - n_prev_chunks)


def _rel_bias(table, q_rows, prev_rows):
    nk = prev_rows + q_rows
    period = q_rows + nk
    j = np.arange(period)
    offset = np.where(j < nk, j, j - period)
    idx = np.clip(prev_rows - offset, -REL_CLIP, REL_CLIP) + REL_CLIP
    v = table[:, idx]
    skew = jnp.tile(v, (1, q_rows))[:, :q_rows * (period - 1)]
    return skew.reshape(table.shape[0], q_rows, period - 1)[:, :, :nk]


def _mixers_prompt(slabs, l, W, relbias, batch, seq):
    za, zb, zc, zd = slabs
    tq = SEQ_TILE
    n_tiles = seq // tq
    cur = lambda b, i: b * n_tiles + i

    def prev(rows, back):
        per_tile = tq // rows
        return lambda b, i: jnp.maximum(b * n_tiles * per_tile + i * per_tile - back,
                                        b * n_tiles * per_tile)

    sink_rows = jnp.broadcast_to(jnp.repeat(W["sink"][l].reshape(2, 1, 4), CHUNK, axis=2),
                                 (2, SUB, 4 * CHUNK))
    kv_a = [((za, za), A_PREV, 512, 768, 2, prev(A_PREV, 1)), ((za, za), tq, 512, 768, 2, cur)]
    part_a = _attention_operands(za, kv_a, sink_rows, tq, cur)
    body_a = functools.partial(_attn_a_cols_kernel, masked_prefix=(A_PREV, tq))

    bias_b = _rel_bias(relbias, tq, B_PREV) * LOG2E + jnp.asarray(
        np.where(_band_mask(tq, B_PREV, 8), 0.0, NEG)[None], F32)
    n_prev_b = B_PREV // tq
    kv_b = [((zb, zb), tq, 512, 1024, 4, prev(tq, n_prev_b - j)) for j in range(n_prev_b)]
    kv_b.append(((zb, zb), tq, 512, 1024, 4, cur))
    part_b = _attention_operands(zb, kv_b, bias_b, tq, cur)
    body_b = functools.partial(_attn_b_kernel, n_kv=len(kv_b), masked_prefix=(B_PREV, tq))

    zero_state = jnp.zeros((batch, C_HEADS, LANES, LANES), F32)
    part_c = _recurrent_operands(zc, _ret_constants(tq), l, W["gn_c"], zero_state, tq, n_tiles)
    part_d = _recurrent_operands(zd, _gla_constants(tq), l, W["gn_d"], zero_state, tq, n_tiles)
    body_d = functools.partial(_gla_kernel, levels=_gla_levels(tq))
    return _mixers_call((part_a, part_b, part_c, part_d), (body_a, body_b, _ret_kernel, body_d),
                        (batch, n_tiles), tq, cur, 60, "mixers")


def _mixers_sample(slabs, l, W, relbias, cache, batch, seq):
    za, zb, zc, zd = slabs
    win_k, win_v, band_k, band_v, s_c0, s_d0 = cache
    cur = lambda b, i: b
    la = win_k.shape[1]
    lb = band_k.shape[2]
    ck = jnp.broadcast_to(win_k[:, :, :, None, :], (batch, la, 2, 2, HD)).reshape(batch * la, 256)
    cv = jnp.broadcast_to(win_v[:, :, :, None, :], (batch, la, 2, 2, HD)).reshape(batch * la, 256)
    kv_a = [((ck, cv), la, 0, 0, 2, cur), ((za, za), seq, 512, 768, 2, cur)]
    part_a = _attention_operands(za, kv_a, _stacked_sinks(W["sink"][l], seq), seq, cur)
    body_a = functools.partial(_attn_a_kernel, n_kv=len(kv_a), chunk_rows=seq,
                               keys_per_chunk=la + seq, masked_prefix=None)

    band = (band_k.reshape(-1, 512), band_v.reshape(-1, 512))
    kv_b = [(band, lb, 0, 0, 4, lambda b, i: l * batch + b), ((zb, zb), seq, 512, 1024, 4, cur)]
    part_b = _attention_operands(zb, kv_b, _rel_bias(relbias, seq, lb) * LOG2E, seq, cur)
    body_b = functools.partial(_attn_b_kernel, n_kv=len(kv_b), masked_prefix=None)

    part_c = _recurrent_operands(zc, _ret_constants(seq), l, W["gn_c"], _state_to_kernel(s_c0),
                                 seq, 1)
    part_d = _recurrent_operands(zd, _gla_constants(seq), l, W["gn_d"], _state_to_kernel(s_d0),
                                 seq, 1)
    body_d = functools.partial(_gla_kernel, levels=_gla_levels(seq))
    return _mixers_call((part_a, part_b, part_c, part_d), (body_a, body_b, _ret_kernel, body_d),
                        (batch, 1), seq, cur, 40, "mixers_step")


def _new_rows(slabs, batch, seq, rows_a, rows_b):
    za = slabs[0].reshape(batch, seq, -1)[:, seq - rows_a:]
    zb = slabs[1].reshape(batch, seq, -1)[:, seq - rows_b:]
    ka = za[:, :, 512:768].reshape(batch, rows_a, 2, 2, HD)[:, :, :, 0, :]
    va = za[:, :, 768:1024].reshape(batch, rows_a, 2, 2, HD)[:, :, :, 0, :]
    kb = zb[:, :, 512:1024].reshape(batch, rows_b, 8, HD)
    vb = zb[:, :, 1024:1536].reshape(batch, rows_b, 8, HD)
    return ka, va, kb, vb


def kernel(x_prompt, x_sample, cache_win_k, cache_win_v, cache_band_k, cache_band_v, state_ret, state_gla, cache_mem_k, cache_mem_v, mem_prompt, norm_mix, w_in, qn_a, kn_a, sink_a, qn_b, kn_b, relbias_b, gn_c, w_alpha_up, b_alpha, gn_d, w_branch, w_merge, w_out, norm_x, norm_mem, w_xq, w_xkv, qn_x, kn_x, w_xo, norm_ffn, w_up, w_down):
    P = dict(norm_mix=norm_mix, w_in=w_in, qn_a=qn_a, kn_a=kn_a, sink_a=sink_a, qn_b=qn_b,
             kn_b=kn_b, relbias_b=relbias_b, gn_c=gn_c, w_alpha_up=w_alpha_up, b_alpha=b_alpha,
             gn_d=gn_d, w_branch=w_branch, w_merge=w_merge, w_out=w_out, norm_x=norm_x,
             norm_mem=norm_mem, w_xq=w_xq, w_xkv=w_xkv, qn_x=qn_x, kn_x=kn_x, w_xo=w_xo,
             norm_ffn=norm_ffn, w_up=w_up, w_down=w_down)
    depth = w_in.shape[0]
    bp, tp, d = x_prompt.shape
    bs, ts, _ = x_sample.shape
    mlen = mem_prompt.shape[1]
    W = _prepare_weights(P)

    cos_p, sin_p = _rope_tables(jnp.arange(tp))
    cos_s, sin_s = _rope_tables(jnp.tile(PAST_LEN + jnp.arange(ts), bs))
    mem_k, mem_v = _memkv(mem_prompt.reshape(bp * mlen, d), W, depth)
    mem_k2 = mem_k.reshape(depth * bp * mlen, 512)
    mem_v2 = mem_v.reshape(depth * bp * mlen, 512)

    xp = x_prompt.reshape(bp * tp, d)
    xs = x_sample.reshape(bs * ts, d)
    tm = SEQ_TILE
    tw = min(TOKEN_TILE, tp)
    n_wide = tp // tw
    rows_p, rows_s = [], []
    st_p, st_s = [], []
    for l in range(depth):
        relbias = relbias_b[l].astype(F32)
        slabs = _inproj(xp, l, W, cos_p, sin_p, tp // tm, tm)
        outs, *states = _mixers_prompt(slabs, l, W, relbias, bp, tp)
        rows_p.append(_new_rows(slabs, bp, tp, min(A_PREV, tp), min(B_PREV, tp)))
        st_p.append(states)
        xp = _merge(xp, outs, l, W, tw)
        xp = _cross_ffn(xp, mem_k2, mem_v2, l, W, (bp, n_wide), tw,
                        lambda b, i: b * n_wide + i, 1, mlen, lambda b, i: l * bp + b)
        slabs = _inproj(xs, l, W, cos_s, sin_s, 1, bs * ts)
        cache = (cache_win_k[l], cache_win_v[l], cache_band_k, cache_band_v,
                 state_ret[l], state_gla[l])
        outs, *states = _mixers_sample(slabs, l, W, relbias, cache, bs, ts)
        rows_s.append(_new_rows(slabs, bs, ts, ts, ts))
        st_s.append(states)
        xs = _merge(xs, outs, l, W, bs * ts)
        xs = _cross_ffn(xs, cache_mem_k.reshape(-1, 512), cache_mem_v.reshape(-1, 512), l, W,
                        (1, 1), bs * ts, lambda b, i: 0, bs, bs * mlen, lambda b, i: l)

    def stack_rows(rows, k):
        return jnp.stack([r[k] for r in rows])

    def stack_state(states, k):
        return jnp.stack([_state_from_kernel(s[k]) for s in states])

    return (xp.reshape(bp, tp, d), xs.reshape(bs, ts, d),
            stack_rows(rows_p, 0), stack_rows(rows_p, 1), stack_rows(rows_p, 2), stack_rows(rows_p, 3),
            stack_state(st_p, 0), stack_state(st_p, 1),
            mem_k.reshape(depth, bp, mlen, 4, LANES), mem_v.reshape(depth, bp, mlen, 4, LANES),
            stack_rows(rows_s, 0), stack_rows(rows_s, 1), stack_rows(rows_s, 2), stack_rows(rows_s, 3),
            stack_state(st_s, 0), stack_state(st_s, 1))
```

```python
import functools

import numpy as np
import jax
import jax.numpy as jnp
from jax import lax
from jax.experimental import pallas as pl
from jax.experimental.pallas import tpu as pltpu

F32 = jnp.float32
BF16 = jnp.bfloat16

EPS = 1e-6
NEG = -1e30
LOG2E = 1.4426950408889634
PAST_LEN = 1024
CHUNK = 64
ROPE_THETA = 10000.0
A_PREV = 2 * CHUNK
B_PREV = 8 * CHUNK
REL_CLIP = 128
GLA_RANK = 16
GLA_TAU = 16.0
C_HEADS = 4
LANES = 128
SUB = 8
HD = 64
SEQ_TILE = 256
TOKEN_TILE = 512
MIB = 2 ** 20


def _dot(a, b):
    return jnp.dot(a.astype(BF16), b.astype(BF16), preferred_element_type=F32)


def _dot_nt(a, b):
    return lax.dot_general(a.astype(BF16), b.astype(BF16), (((1,), (1,)), ((), ())),
                           preferred_element_type=F32)


def _dot_tn(a, b):
    return lax.dot_general(a.astype(BF16), b.astype(BF16), (((0,), (0,)), ((), ())),
                           preferred_element_type=F32)


def _split3(x):
    x1 = x.astype(BF16)
    r1 = x - x1.astype(F32)
    x2 = r1.astype(BF16)
    x3 = (r1 - x2.astype(F32)).astype(BF16)
    return x1, x2, x3


def _dot_01_left(m01, x):
    x1, x2, x3 = _split3(x)
    dot = functools.partial(jnp.dot, preferred_element_type=F32)
    return dot(m01, x1) + dot(m01, x2) + dot(m01, x3)


def _rms_rows(x, g):
    return x * lax.rsqrt(jnp.mean(x * x, axis=-1, keepdims=True) + EPS) * g


def _lane_iota(shape):
    return lax.broadcasted_iota(jnp.int32, shape, len(shape) - 1)


def _group_ones(width):
    r = lax.broadcasted_iota(jnp.int32, (width, width), 0) // HD
    c = lax.broadcasted_iota(jnp.int32, (width, width), 1) // HD
    return jnp.where(r == c, 1.0, 0.0).astype(BF16)


def _head64_norm(x, g, ones):
    sq = x * x
    hi = sq.astype(BF16)
    lo = (sq - hi.astype(F32)).astype(BF16)
    ss = (jnp.dot(hi, ones, preferred_element_type=F32)
          + jnp.dot(lo, ones, preferred_element_type=F32))
    return x * lax.rsqrt(ss * (1.0 / HD) + EPS) * g


def _rope64(x, cos, sin):
    first_half = (_lane_iota((1, LANES)) % HD) < (HD // 2)
    swapped = jnp.where(first_half, pltpu.roll(x, LANES - HD // 2, 1), pltpu.roll(x, HD // 2, 1))
    return x * cos + swapped * sin


def _sigmoid(x):
    return 1.0 / (1.0 + jnp.exp(-x))


def _log_sigmoid(x):
    return jnp.minimum(x, 0.0) - jnp.log(1.0 + jnp.exp(-jnp.abs(x)))


def _params(n_axes, vmem_mib):
    return pltpu.CompilerParams(dimension_semantics=("arbitrary",) * n_axes,
                                vmem_limit_bytes=vmem_mib * MIB)


def _const_spec(shape, index):
    return pl.BlockSpec(shape, lambda *_: index, pipeline_mode=pl.Buffered(1))


def _inproj_stages(x_ref, g_ref, wa_ref, wb_ref, wc_ref, wd_ref, wup_ref, ba_ref,
                   gqa_ref, gka_ref, gqb_ref, gkb_ref, cos_ref, sin_ref, dst):
    h = _rms_rows(x_ref[...], g_ref[...]).astype(BF16)
    cos = cos_ref[...]
    sin = sin_ref[...]
    pair = 2 * LANES
    ones = _group_ones(pair)

    za = jnp.dot(h, wa_ref[...], preferred_element_type=F32)
    zb = jnp.dot(h, wb_ref[...], preferred_element_type=F32)
    yield
    for p in range(3):
        sl = slice(p * pair, (p + 1) * pair)
        gain = gqa_ref[:, sl] if p < 2 else gka_ref[...]
        y = _head64_norm(za[:, sl], gain, ones)
        ref, col0 = (dst["a_q"], p * pair) if p < 2 else (dst["a_k"], 0)
        for c in range(2):
            ref[:, col0 + c * LANES:col0 + (c + 1) * LANES] = _rope64(
                y[:, c * LANES:(c + 1) * LANES], cos, sin)
    dst["a_v"][...] = za[:, 768:1024]
    zc = jnp.dot(h, wc_ref[...], preferred_element_type=F32)
    yield
    for p in range(4):
        sl = slice(p * pair, (p + 1) * pair)
        if p < 2:
            dst["b_q"][:, sl] = _head64_norm(zb[:, sl], gqb_ref[:, sl], ones)
        else:
            ksl = slice((p - 2) * pair, (p - 1) * pair)
            dst["b_k"][:, ksl] = _head64_norm(zb[:, sl], gkb_ref[:, ksl], ones)
    dst["b_v"][...] = zb[:, 1024:1536]
    zd = jnp.dot(h, wd_ref[...], preferred_element_type=F32)
    yield
    oc_ref = dst["c"]
    for c in range(2):
        sl = slice(c * LANES, (c + 1) * LANES)
        oc_ref[:, sl] = _rope64(zc[:, sl], cos, sin)
    for c in range(2):
        sl = slice(256 + c * LANES, 256 + (c + 1) * LANES)
        oc_ref[:, sl] = _rope64(zc[:, sl], cos, sin) * (HD ** -0.5)
    oc_ref[:, 512:1536] = zc[:, 512:1536]
    yield
    od_ref = dst["d"]
    od_ref[:, 0:256] = zd[:, 0:256] * (HD ** -0.5)
    od_ref[:, 256:1536] = zd[:, 256:1536]
    pre = _dot(zd[:, 1536:1664], wup_ref[...]) + ba_ref[...]
    od_ref[:, 1536:1792] = _log_sigmoid(pre) * (LOG2E / GLA_TAU)


def _inproj_kernel(*refs):
    *ins, oa_ref, ob_ref, oc_ref, od_ref = refs
    dst = dict(a_q=oa_ref.at[:, 0:512], a_k=oa_ref.at[:, 512:768], a_v=oa_ref.at[:, 768:1024],
               b_q=ob_ref.at[:, 0:512], b_k=ob_ref.at[:, 512:1024], b_v=ob_ref.at[:, 1024:1536],
               c=oc_ref, d=od_ref)
    for _ in _inproj_stages(*ins, dst):
        pass


def _inproj_specs(l, d, tm, n_pos_blocks, row_index):
    lay = (l, 0, 0)
    return [
        pl.BlockSpec((tm, d), lambda *g: (row_index(*g), 0)),
        _const_spec((None, 1, d), lay),
        _const_spec((None, d, 1024), lay),
        _const_spec((None, d, 1536), lay),
        _const_spec((None, d, 1536), lay),
        _const_spec((None, d, 1664), lay),
        _const_spec((None, LANES, 256), lay),
        _const_spec((None, 1, 256), lay),
        _const_spec((None, 1, 512), lay),
        _const_spec((None, 1, 256), lay),
        _const_spec((None, 1, 512), lay),
        _const_spec((None, 1, 512), lay),
        pl.BlockSpec((tm, LANES), lambda *g: (row_index(*g) % n_pos_blocks, 0)),
        pl.BlockSpec((tm, LANES), lambda *g: (row_index(*g) % n_pos_blocks, 0)),
    ]


def _inproj_args(x, W, cos, sin):
    return [x, W["norm_mix"], W["wa"], W["wb"], W["wc"], W["wd"], W["wup"], W["b_alpha"],
            W["gqa"], W["gka"], W["gqb"], W["gkb"], cos, sin]


def _inproj(x, l, W, cos, sin, n_pos_blocks, tm):
    n, d = x.shape
    row = lambda i: (i, 0)
    widths = (1024, 1536, 1536, 1792)
    return pl.pallas_call(
        _inproj_kernel,
        grid=(n // tm,),
        in_specs=_inproj_specs(l, d, tm, n_pos_blocks, lambda i: i),
        out_specs=[pl.BlockSpec((tm, w), row) for w in widths],
        out_shape=[jax.ShapeDtypeStruct((n, w), F32) for w in widths],
        compiler_params=_params(1, 56),
        name="inproj",
    )(*_inproj_args(x, W, cos, sin))


def _stack_rows(refs):
    blocks = [r[...].astype(BF16) for r in refs]
    return blocks[0] if len(blocks) == 1 else jnp.concatenate(blocks, axis=0)


def _first_valid_key(masked_prefix):
    prev_rows, seq_tile = masked_prefix
    return prev_rows - pl.program_id(1) * seq_tile


def _attn_a_kernel(*refs, n_kv, chunk_rows, keys_per_chunk, masked_prefix):
    q_ref = refs[0]
    k_all = _stack_rows(refs[1:1 + n_kv])
    v_all = _stack_rows(refs[1 + n_kv:1 + 2 * n_kv])
    sink_ref = refs[1 + 2 * n_kv]
    o_ref = refs[2 + 2 * n_kv]
    cr = chunk_rows
    low = _lane_iota((1, LANES)) < HD
    units = [(j, g) for j in range(q_ref.shape[0] // cr) for g in range(2)]
    keys_of = lambda j: slice(j * cr, j * cr + keys_per_chunk)
    group = lambda g: slice(g * LANES, (g + 1) * LANES)
    scores = []
    for j, g in units:
        stacked = []
        for c in (2 * g, 2 * g + 1):
            q_c = q_ref[j * cr:(j + 1) * cr, c * LANES:(c + 1) * LANES]
            stacked += [jnp.where(low, q_c, 0.0), jnp.where(low, 0.0, q_c)]
        s = _dot_nt(jnp.concatenate(stacked, axis=0), k_all[keys_of(j), group(g)])
        if masked_prefix is not None:
            key_ok = _lane_iota((1, keys_per_chunk)) >= _first_valid_key(masked_prefix) - j * cr
            s = jnp.where(key_ok, s, NEG)
        scores.append(s)
    yield
    probs = []
    for (j, g), s in zip(units, scores):
        sink = sink_ref[g][:, 0:1]
        m = jnp.maximum(jnp.max(s, axis=-1, keepdims=True), sink)
        p = jnp.exp2(s - m)
        probs.append((p.astype(BF16), jnp.sum(p, axis=-1, keepdims=True) + jnp.exp2(sink - m)))
    yield
    outs = [_dot(p, v_all[keys_of(j), group(g)]) / den for (j, g), (p, den) in zip(units, probs)]
    yield
    for (j, g), o in zip(units, outs):
        for t, c in enumerate((2 * g, 2 * g + 1)):
            o_ref[j * cr:(j + 1) * cr, c * LANES:(c + 1) * LANES] = jnp.where(
                low, o[2 * t * cr:(2 * t + 1) * cr],
                o[(2 * t + 1) * cr:(2 * t + 2) * cr]).astype(o_ref.dtype)


def _attn_a_cols_kernel(q_ref, kp_ref, kc_ref, vp_ref, vc_ref, sink_ref, o_ref, *, masked_prefix):
    prev_rows = kp_ref.shape[0]
    k_all = _stack_rows((kp_ref, kc_ref))
    v_all = _stack_rows((vp_ref, vc_ref))
    cr = CHUNK
    keys_per_chunk = prev_rows + cr
    lane_low = _lane_iota((1, LANES)) < HD
    same_half = ((lax.broadcasted_iota(jnp.int32, (LANES, LANES), 0) < HD)
                 == (_lane_iota((LANES, LANES)) < HD))
    key_row = lax.broadcasted_iota(jnp.int32, (keys_per_chunk, 2 * LANES), 0)
    units = [(j, g) for j in range(q_ref.shape[0] // cr) for g in range(2)]
    keys_of = lambda j: slice(j * cr, j * cr + keys_per_chunk)
    group = lambda g: slice(g * LANES, (g + 1) * LANES)

    scores = []
    for j, g in units:
        q_t = []
        for c in (2 * g, 2 * g + 1):
            q_c = q_ref[j * cr:(j + 1) * cr, c * LANES:(c + 1) * LANES]
            both = jnp.concatenate([q_c, q_c], axis=0).T
            q_t.append(jnp.where(same_half, both, 0.0).astype(BF16))
        s_t = jnp.dot(k_all[keys_of(j), group(g)], jnp.concatenate(q_t, axis=1),
                      preferred_element_type=F32)
        if j * cr < prev_rows:
            s_t = jnp.where(key_row >= _first_valid_key(masked_prefix) - j * cr, s_t, NEG)
        scores.append(s_t)
    yield
    probs = []
    for (j, g), s_t in zip(units, scores):
        sink = sink_ref[g][0:1, :]
        m = jnp.maximum(jnp.max(s_t, axis=0, keepdims=True), sink)
        p_t = jnp.exp2(s_t - m)
        den = jnp.sum(p_t, axis=0, keepdims=True) + jnp.exp2(sink - m)
        probs.append((p_t.astype(BF16), den))
    yield
    outs = []
    for (j, g), (p_t, den) in zip(units, probs):
        outs.append(_dot_tn(v_all[keys_of(j), group(g)], p_t) / den)
    yield
    for (j, g), o_t in zip(units, outs):
        for t, c in enumerate((2 * g, 2 * g + 1)):
            blk = o_t[:, t * LANES:(t + 1) * LANES].T
            o_ref[j * cr:(j + 1) * cr, c * LANES:(c + 1) * LANES] = jnp.where(
                lane_low, blk[0:cr], blk[cr:2 * cr]).astype(o_ref.dtype)


def _attn_b_kernel(*refs, n_kv, masked_prefix):
    q_ref = refs[0]
    k_all = _stack_rows(refs[1:1 + n_kv])
    v_all = _stack_rows(refs[1 + n_kv:1 + 2 * n_kv])
    bias_ref = refs[1 + 2 * n_kv]
    o_ref = refs[2 + 2 * n_kv]
    low = _lane_iota((1, LANES)) < HD
    if masked_prefix is not None:
        key_ok = _lane_iota((1, k_all.shape[0])) >= _first_valid_key(masked_prefix)
    slab = lambda c: slice(c * LANES, (c + 1) * LANES)
    heads = [(c, half) for c in range(4) for half in range(2)]
    scores = []
    for c, half in heads:
        q_c = q_ref[:, slab(c)]
        q_h = jnp.where(low, q_c, 0.0) if half == 0 else jnp.where(low, 0.0, q_c)
        scores.append(_dot_nt(q_h, k_all[:, slab(c)]))
    yield
    probs = []
    for (c, half), s in zip(heads, scores):
        s = s + bias_ref[2 * c + half]
        if masked_prefix is not None:
            s = jnp.where(key_ok, s, NEG)
        p = jnp.exp2(s - jnp.max(s, axis=-1, keepdims=True))
        probs.append((p.astype(BF16), jnp.sum(p, axis=-1, keepdims=True)))
    yield
    outs = [_dot(p, v_all[:, slab(c)]) / den for (c, half), (p, den) in zip(heads, probs)]
    yield
    for c in range(4):
        o_ref[:, slab(c)] = jnp.where(low, outs[2 * c], outs[2 * c + 1]).astype(o_ref.dtype)


def _attention_operands(q_arr, kv_arrs, extra, q_rows, q_index):
    in_specs = [pl.BlockSpec((q_rows, 512), lambda b, i: (q_index(b, i), 0))]
    args = [q_arr]
    for use_v in (False, True):
        for arrs, rows, kc0, vc0, ncol, index_fn in kv_arrs:
            width = ncol * LANES
            col_block = (vc0 if use_v else kc0) // width
            in_specs.append(pl.BlockSpec(
                (rows, width),
                functools.partial(lambda b, i, f, cb: (f(b, i), cb), f=index_fn, cb=col_block)))
            args.append(arrs[1] if use_v else arrs[0])
    in_specs.append(_const_spec(extra.shape, (0,) * extra.ndim))
    args.append(extra)
    return in_specs, args


def _stacked_sinks(sink, chunk_rows):
    col = jnp.repeat(sink.reshape(2, 4), chunk_rows, axis=1)
    return jnp.broadcast_to(col[:, :, None], (2, 4 * chunk_rows, LANES)).astype(F32)


def _head_lane_mask(head):
    low = _lane_iota((1, LANES)) < HD
    return low if head % 2 == 0 else jnp.logical_not(low)


def _ret_kernel(z_ref, dmat_ref, qdec_ref, kdec_ref, sdec_ref, gn_ref, o_ref, st_ref):
    q = z_ref[:, 0:256]
    k = z_ref[:, 256:512]
    q_in = q * qdec_ref[...]
    k_st = k * kdec_ref[...]
    heads = range(C_HEADS)
    pair = lambda h: slice((h // 2) * LANES, (h // 2 + 1) * LANES)
    v = [z_ref[:, 512 + h * LANES:512 + (h + 1) * LANES].astype(BF16) for h in heads]
    old_state = [st_ref[h] for h in heads]
    raw = [_dot_nt(q[:, pair(h)], jnp.where(_head_lane_mask(h), k[:, pair(h)], 0.0)) for h in heads]
    inter = [_dot_nt(q_in[:, pair(h)], old_state[h]) for h in heads]
    state_upd = [_dot_tn(v[h], jnp.where(_head_lane_mask(h), k_st[:, pair(h)], 0.0)) for h in heads]
    yield
    scores = [(raw[h] * dmat_ref[h]).astype(BF16) for h in heads]
    yield
    outs = [jnp.dot(scores[h], v[h], preferred_element_type=F32) + inter[h] for h in heads]
    yield
    for h in heads:
        o = outs[h]
        g_h = z_ref[:, 1024 + h * LANES:1024 + (h + 1) * LANES]
        oc = o - jnp.mean(o, axis=-1, keepdims=True)
        y = oc * lax.rsqrt(jnp.mean(oc * oc, axis=-1, keepdims=True) + EPS)
        y = y * gn_ref[:, h * LANES:(h + 1) * LANES]
        o_ref[:, h * LANES:(h + 1) * LANES] = (y * (g_h * _sigmoid(g_h))).astype(o_ref.dtype)
        st_ref[h] = old_state[h] * sdec_ref[:, pair(h)] + state_upd[h]


def _gla_kernel(z_ref, tril_ref, lmask_ref, bd_ref, ind_ref, rep_ref, gn_ref, o_ref, st_ref, *,
                levels):
    tc = z_ref.shape[0]
    diag = bd_ref.shape[0]
    q = z_ref[:, 0:256]
    k = z_ref[:, 256:512]
    la = z_ref[:, 1536:1792]
    cum = _dot_01_left(tril_ref[...], la)
    cum_last = cum[tc - 1:tc, :]
    yield

    off_diag = []
    for size in (s for s in levels if s > diag):
        for start in range(0, tc, size):
            m = start + size // 2 - 1
            rows = slice(start + size // 2, start + size)
            cols = slice(start, start + size // 2)
            q_r = q[rows] * jnp.exp2(cum[rows] - cum[m:m + 1])
            k_c = k[cols] * jnp.exp2(cum[m:m + 1] - cum[cols])
            off_diag.append((rows, cols, q_r, k_c))
    small = [s for s in levels if s <= diag]
    q_lv, k_lv = [], []
    for size in small:
        mids = []
        for start in range(0, tc, size):
            m = start + size // 2 - 1
            mids.append(jnp.broadcast_to(cum[m:m + 1, :], (size, 256)))
        cm = mids[0] if len(mids) == 1 else jnp.concatenate(mids, axis=0)
        e = jnp.exp2(-jnp.abs(cum - cm))
        q_lv.append(q * e)
        k_lv.append(k * e)

    nb = tc // SUB
    q3 = q.reshape(nb, SUB, 256)
    k3 = k.reshape(nb, SUB, 256)
    c3 = cum.reshape(nb, SUB, 256)
    row_in_block = lax.broadcasted_iota(jnp.int32, (1, SUB, 1), 1)
    r = None
    for j in range(SUB):
        e = jnp.exp2(jnp.where(row_in_block >= j, c3 - c3[:, j:j + 1, :], NEG))
        t = q3 * k3[:, j:j + 1, :] * e
        part = jnp.dot(t.reshape(tc, 256).astype(BF16), ind_ref[j], preferred_element_type=F32)
        r = part if r is None else r + part
    r = r.astype(BF16)
    bd = bd_ref[...]
    yield

    q_in = q * jnp.exp2(cum)
    k_st = k * jnp.exp2(cum_last - cum)
    s_dec = jnp.exp2(cum_last)

    heads = range(C_HEADS)
    pair = lambda h: slice((h // 2) * LANES, (h // 2 + 1) * LANES)
    block = lambda b: slice(b * diag, (b + 1) * diag)
    n_blocks = tc // diag
    v = [z_ref[:, 512 + h * LANES:512 + (h + 1) * LANES].astype(BF16) for h in heads]
    old_state = [st_ref[h] for h in heads]
    inter = [_dot_nt(q_in[:, pair(h)], old_state[h]) for h in heads]
    state_upd = [_dot_tn(v[h], jnp.where(_head_lane_mask(h), k_st[:, pair(h)], 0.0)) for h in heads]
    raw = {}
    for h in heads:
        hm = _head_lane_mask(h)
        for b in range(n_blocks):
            raw[h, b, "blk8"] = jnp.dot(r[block(b)], rep_ref[h], preferred_element_type=F32)
            for lv in range(len(small)):
                k_h = jnp.where(hm, k_lv[lv][block(b), pair(h)], 0.0)
                raw[h, b, lv] = _dot_nt(q_lv[lv][block(b), pair(h)], k_h)
        for n, (rows, cols, q_r, k_c) in enumerate(off_diag):
            raw[h, "off", n] = _dot_nt(q_r[:, pair(h)], jnp.where(hm, k_c[:, pair(h)], 0.0))
    yield
    scores = {}
    for h in heads:
        for b in range(n_blocks):
            s_b = raw[h, b, "blk8"] * bd
            for lv in range(len(small)):
                s_b = s_b + raw[h, b, lv] * lmask_ref[lv]
            scores[h, b] = s_b.astype(BF16)
    yield
    outs = []
    for h in heads:
        blocks = [jnp.dot(scores[h, b], v[h][block(b)], preferred_element_type=F32)
                  for b in range(n_blocks)]
        for n, (rows, cols, _, _) in enumerate(off_diag):
            o_rc = _dot(raw[h, "off", n], v[h][cols])
            for b in range(rows.start // diag, rows.stop // diag):
                lo = b * diag - rows.start
                blocks[b] = blocks[b] + o_rc[lo:lo + diag]
        o = blocks[0] if n_blocks == 1 else jnp.concatenate(blocks, axis=0)
        outs.append(o + inter[h])
    yield
    for h in heads:
        o = outs[h]
        g_h = z_ref[:, 1024 + h * LANES:1024 + (h + 1) * LANES]
        y = o * lax.rsqrt(jnp.mean(o * o, axis=-1, keepdims=True) + EPS)
        y = y * gn_ref[:, h * LANES:(h + 1) * LANES]
        o_ref[:, h * LANES:(h + 1) * LANES] = (y * (g_h * _sigmoid(g_h))).astype(o_ref.dtype)
        st_ref[h] = old_state[h] * s_dec[:, pair(h)] + state_upd[h]


def _ret_constants(tc):
    log_gamma = np.log1p(-np.exp2(-5.0 - np.arange(C_HEADS, dtype=np.float64)))
    idx = np.arange(tc)
    diff = idx[:, None] - idx[None, :]
    dmat = np.where(diff >= 0, np.exp(log_gamma[:, None, None] * np.maximum(diff, 0)[None]), 0.0)
    per_lane = np.repeat(log_gamma, HD)
    qdec = np.exp((idx[:, None] + 1) * per_lane[None, :])
    kdec = np.exp((tc - 1 - idx[:, None]) * per_lane[None, :])
    sdec = np.exp(tc * per_lane)[None, :]
    return [jnp.asarray(a, F32) for a in (dmat, qdec, kdec, sdec)]


def _gla_levels(tc):
    levels = []
    size = tc
    while size >= 2 * SUB:
        levels.append(size)
        size //= 2
    return tuple(levels)


def _gla_diag(tc):
    return min(tc, LANES)


def _gla_constants(tc):
    diag = _gla_diag(tc)
    idx = np.arange(tc)
    tril = (idx[None, :] <= idx[:, None]).astype(np.float32)
    idx = np.arange(diag)
    i, j = idx[:, None], idx[None, :]
    lmasks = []
    for size in (s for s in _gla_levels(tc) if s <= diag):
        lmasks.append(((i // size == j // size) & (i % size >= size // 2) & (j % size < size // 2)))
    lmask = np.stack(lmasks).astype(np.float32)
    bd = (i // SUB == j // SUB).astype(np.float32)
    ind = np.zeros((SUB, 256, LANES), np.float32)
    for jj in range(SUB):
        for h in range(C_HEADS):
            ind[jj, h * HD:(h + 1) * HD, h * SUB + jj] = 1.0
    rep = np.zeros((C_HEADS, LANES, diag), np.float32)
    for h in range(C_HEADS):
        for jj in range(SUB):
            rep[h, h * SUB + jj, jj::SUB] = 1.0
    return (jnp.asarray(tril, BF16), jnp.asarray(lmask, F32), jnp.asarray(bd, F32),
            jnp.asarray(ind, BF16), jnp.asarray(rep, BF16))


def _state_spec():
    return pl.BlockSpec((None, C_HEADS, LANES, LANES), lambda b, i: (b, 0, 0, 0))


def _recurrent_operands(z, consts, l, gn, s0, tc, n_tiles):
    in_specs = [pl.BlockSpec((tc, z.shape[1]), lambda b, i: (b * n_tiles + i, 0))]
    in_specs += [_const_spec(c.shape, (0,) * c.ndim) for c in consts]
    in_specs += [_const_spec((None, 1, 512), (l, 0, 0)), _state_spec()]
    return in_specs, [z, *consts, gn, s0]


def _mixers_kernel(*refs, bodies, n_in):
    ends = np.cumsum(n_in)
    a_in, b_in, c_in, d_in = (refs[e - n:e] for e, n in zip(ends, n_in))
    o_ref, sc_ref, sd_ref, stc_ref, std_ref = refs[ends[-1]:]
    *c_in, s0c_ref = c_in
    *d_in, s0d_ref = d_in
    i = pl.program_id(1)

    @pl.when(i == 0)
    def _():
        stc_ref[...] = s0c_ref[...]
        std_ref[...] = s0d_ref[...]

    body_a, body_b, body_c, body_d = bodies
    running = [body_d(*d_in, o_ref.at[:, 1536:2048], std_ref),
               body_b(*b_in, o_ref.at[:, 512:1024]),
               body_a(*a_in, o_ref.at[:, 0:512]),
               body_c(*c_in, o_ref.at[:, 1024:1536], stc_ref)]
    while running:
        for gen in list(running):
            if next(gen, "done") == "done":
                running.remove(gen)

    @pl.when(i == pl.num_programs(1) - 1)
    def _():
        sc_ref[...] = stc_ref[...]
        sd_ref[...] = std_ref[...]


def _mixers_call(parts, bodies, grid, rows, row_index, vmem_mib, name):
    in_specs = [s for specs, _ in parts for s in specs]
    args = [a for _, arrs in parts for a in arrs]
    n_in = tuple(len(specs) for specs, _ in parts)
    n_rows = grid[0] * grid[1] * rows
    state_shape = jax.ShapeDtypeStruct((grid[0], C_HEADS, LANES, LANES), F32)
    return pl.pallas_call(
        functools.partial(_mixers_kernel, bodies=bodies, n_in=n_in),
        grid=grid,
        in_specs=in_specs,
        out_specs=[pl.BlockSpec((rows, 2048), lambda b, i: (row_index(b, i), 0)),
                   _state_spec(), _state_spec()],
        out_shape=[jax.ShapeDtypeStruct((n_rows, 2048), BF16), state_shape, state_shape],
        scratch_shapes=[pltpu.VMEM((C_HEADS, LANES, LANES), F32)] * 2,
        compiler_params=_params(2, vmem_mib),
        name=name,
    )(*args)


def _state_to_kernel(s):
    st = jnp.swapaxes(s, -1, -2)
    lo = jnp.pad(st, ((0, 0), (0, 0), (0, 0), (0, HD)))
    hi = jnp.pad(st, ((0, 0), (0, 0), (0, 0), (HD, 0)))
    odd = (jnp.arange(C_HEADS) % 2 == 1)[None, :, None, None]
    return jnp.where(odd, hi, lo)


def _state_from_kernel(st):
    lo = st[..., :HD]
    hi = st[..., HD:]
    odd = (jnp.arange(C_HEADS) % 2 == 1)[None, :, None, None]
    return jnp.swapaxes(jnp.where(odd, hi, lo), -1, -2)


def _merge_kernel(x_ref, g_ref, o_ref, wm_ref, wbr_ref, wo_ref, y_ref):
    x = x_ref[...]
    d = x.shape[1]
    h = _rms_rows(x, g_ref[...]).astype(BF16)
    acc = None
    for b in range(4):
        gate = _sigmoid(jnp.dot(h, wm_ref[:, b * d:(b + 1) * d], preferred_element_type=F32))
        term = gate * jnp.dot(o_ref[:, b * 512:(b + 1) * 512], wbr_ref[b],
                              preferred_element_type=F32)
        acc = term if acc is None else acc + term
    y_ref[...] = x + _dot(acc, wo_ref[...])


def _merge(x, outs, l, W, tm):
    n, d = x.shape
    row = lambda i: (i, 0)
    lay = (l, 0, 0)
    in_specs = [pl.BlockSpec((tm, d), row), _const_spec((None, 1, d), lay),
                pl.BlockSpec((tm, 2048), row),
                _const_spec((None, d, 4 * d), lay),
                _const_spec((None, 4, 512, d), (l, 0, 0, 0)),
                _const_spec((None, d, d), lay)]
    return pl.pallas_call(
        _merge_kernel,
        grid=(n // tm,),
        in_specs=in_specs,
        out_specs=pl.BlockSpec((tm, d), row),
        out_shape=jax.ShapeDtypeStruct((n, d), F32),
        compiler_params=_params(1, 48),
        name="merge",
    )(x, W["norm_mix"], outs, W["w_merge"], W["w_branch"], W["w_out"])


def _cross_ffn_kernel(x_ref, gx_ref, wq_ref, qn_ref, mk_ref, mv_ref, wxo_ref, gf_ref, wup_ref,
                      wdn_ref, y_ref, *, n_seq):
    x = x_ref[...]
    d = x.shape[1]
    rows = x.shape[0] // n_seq
    mlen = mk_ref.shape[0] // n_seq
    hx = _rms_rows(x, gx_ref[...])
    q = _dot(hx, wq_ref[...])
    slab = lambda h: slice(h * LANES, (h + 1) * LANES)
    units = [(h, b) for h in range(4) for b in range(n_seq)]
    mem_of = lambda b: slice(b * mlen, (b + 1) * mlen)
    q_n = [(_rms_rows(q[:, slab(h)], qn_ref[...]) * (LANES ** -0.5 * LOG2E)).astype(BF16)
           for h in range(4)]
    scores = [_dot_nt(q_n[h][b * rows:(b + 1) * rows], mk_ref[mem_of(b), slab(h)])
              for h, b in units]
    probs = []
    for s in scores:
        p = jnp.exp2(s - jnp.max(s, axis=-1, keepdims=True))
        probs.append((p.astype(BF16), jnp.sum(p, axis=-1, keepdims=True)))
    outs = [_dot(p, mv_ref[mem_of(b), slab(h)]) / den for (h, b), (p, den) in zip(units, probs)]
    heads = []
    for h in range(4):
        per_seq = outs[h * n_seq:(h + 1) * n_seq]
        heads.append(per_seq[0] if n_seq == 1 else jnp.concatenate(per_seq, axis=0))
    x = x + _dot(jnp.concatenate(heads, axis=1), wxo_ref[...])
    hf = _rms_rows(x, gf_ref[...]).astype(BF16)
    acc = None
    for j in range(wup_ref.shape[1] // d):
        sl = slice(j * d, (j + 1) * d)
        u = jnp.maximum(jnp.dot(hf, wup_ref[:, sl], preferred_element_type=F32), 0.0)
        term = _dot(u * u, wdn_ref[sl, :])
        acc = term if acc is None else acc + term
    y_ref[...] = x + acc


def _cross_ffn(x, mk, mv, l, W, grid, rows, x_index, n_seq, mem_rows, mem_index):
    n, d = x.shape
    lay = (l, 0, 0)
    dff = W["w_up"].shape[-1]
    in_specs = [pl.BlockSpec((rows, d), lambda b, i: (x_index(b, i), 0)),
                _const_spec((None, 1, d), lay),
                _const_spec((None, d, 512), lay),
                _const_spec((None, 1, LANES), lay),
                pl.BlockSpec((mem_rows, 512), lambda b, i: (mem_index(b, i), 0)),
                pl.BlockSpec((mem_rows, 512), lambda b, i: (mem_index(b, i), 0)),
                _const_spec((None, 512, d), lay),
                _const_spec((None, 1, d), lay),
                _const_spec((None, d, dff), lay),
                _const_spec((None, dff, d), lay)]
    return pl.pallas_call(
        functools.partial(_cross_ffn_kernel, n_seq=n_seq),
        grid=grid,
        in_specs=in_specs,
        out_specs=pl.BlockSpec((rows, d), lambda b, i: (x_index(b, i), 0)),
        out_shape=jax.ShapeDtypeStruct((n, d), F32),
        compiler_params=_params(2, 48),
        name="cross_ffn",
    )(x, W["norm_x"], W["w_xq"], W["qn_x"], mk, mv, W["w_xo"], W["norm_ffn"], W["w_up"],
      W["w_down"])


def _memkv_kernel(mem_ref, g_ref, w_ref, kn_ref, k_ref, v_ref):
    hm = _rms_rows(mem_ref[...], g_ref[...])
    kv = _dot(hm, w_ref[...])
    for h in range(4):
        sl = slice(h * LANES, (h + 1) * LANES)
        k_ref[:, sl] = _rms_rows(kv[:, sl], kn_ref[...])
    v_ref[...] = kv[:, 512:1024]


def _memkv(mem, W, depth):
    n, d = mem.shape
    mlen = 256
    batch = n // mlen
    lay = lambda l, b: (l, 0, 0)
    return pl.pallas_call(
        _memkv_kernel,
        grid=(depth, batch),
        in_specs=[pl.BlockSpec((mlen, d), lambda l, b: (b, 0)),
                  pl.BlockSpec((None, 1, d), lay),
                  pl.BlockSpec((None, d, 1024), lay),
                  pl.BlockSpec((None, 1, LANES), lay)],
        out_specs=[pl.BlockSpec((None, mlen, 512), lambda l, b: (l, b, 0))] * 2,
        out_shape=[jax.ShapeDtypeStruct((depth, n, 512), F32)] * 2,
        compiler_params=_params(2, 32),
        name="memkv",
    )(mem, W["norm_mem"], W["w_xkv"], W["kn_x"])


def _front_kernel(*refs, n_in, bodies, prev_a, prev_b):
    ends = np.cumsum(n_in)
    proj_in, a_in, b_in, c_in, d_in = (refs[e - n:e] for e, n in zip(ends, n_in))
    (o_ref, kva_ref, kvb_ref, sc_ref, sd_ref,
     qa_s, hist_a, qb_s, hist_b, zc_s, zd_s, stc_ref, std_ref) = refs[ends[-1]:]
    *c_in, s0c_ref = c_in
    *d_in, s0d_ref = d_in
    tq = qa_s.shape[0]
    i = pl.program_id(1)

    @pl.when(i == 0)
    def _():
        hist_a[0:prev_a, :] = jnp.zeros((prev_a, hist_a.shape[1]), F32)
        hist_b[0:prev_b, :] = jnp.zeros((prev_b, hist_b.shape[1]), F32)
        stc_ref[...] = s0c_ref[...]
        std_ref[...] = s0d_ref[...]

    cur_a = slice(prev_a, prev_a + tq)
    cur_b = slice(prev_b, prev_b + tq)
    dst = dict(a_q=qa_s, a_k=hist_a.at[cur_a, 0:256], a_v=hist_a.at[cur_a, 256:512],
               b_q=qb_s, b_k=hist_b.at[cur_b, 0:512], b_v=hist_b.at[cur_b, 512:1024],
               c=zc_s, d=zd_s)
    body_a, body_b, body_c, body_d = bodies
    (sink_ref,), (bias_ref,) = a_in, b_in
    n_prev_b = prev_b // tq
    k_b = [hist_b.at[j * tq:(j + 1) * tq, 0:512] for j in range(n_prev_b + 1)]
    v_b = [hist_b.at[j * tq:(j + 1) * tq, 512:1024] for j in range(n_prev_b + 1)]
    starts = {
        2: lambda: body_a(qa_s, hist_a.at[0:prev_a, 0:256], hist_a.at[cur_a, 0:256],
                          hist_a.at[0:prev_a, 256:512], hist_a.at[cur_a, 256:512], sink_ref,
                          o_ref.at[:, 0:512]),
        3: lambda: body_b(qb_s, *k_b, *v_b, bias_ref, o_ref.at[:, 512:1024]),
        4: lambda: body_c(zc_s, *c_in, o_ref.at[:, 1024:1536], stc_ref),
        5: lambda: body_d(zd_s, *d_in, o_ref.at[:, 1536:2048], std_ref),
    }
    proj = _inproj_stages(*proj_in, dst)
    running = [proj]
    proj_stage = 0
    while running:
        for gen in list(running):
            if next(gen, "done") == "done":
                running.remove(gen)
            if gen is proj:
                proj_stage += 1
                if proj_stage in starts:
                    running.append(starts[proj_stage]())

    low = _lane_iota((1, LANES)) < HD
    for t, col0 in enumerate((0, 256)):
        kva_ref[:, t * LANES:(t + 1) * LANES] = jnp.where(
            low, hist_a[cur_a, col0:col0 + LANES], hist_a[cur_a, col0 + LANES:col0 + 2 * LANES])
    kvb_ref[...] = hist_b[cur_b, :]
    hist_a[0:prev_a, :] = hist_a[tq:tq + prev_a, :]
    hist_b[0:prev_b, :] = hist_b[tq:tq + prev_b, :]

    @pl.when(i == pl.num_programs(1) - 1)
    def _():
        sc_ref[...] = stc_ref[...]
        sd_ref[...] = std_ref[...]


def _front(x, l, W, cos, sin, relbias, batch, seq):
    n, d = x.shape
    tq = SEQ_TILE
    n_tiles = seq // tq
    cur = lambda b, i: b * n_tiles + i
    sink_rows = jnp.broadcast_to(jnp.repeat(W["sink"][l].reshape(2, 1, 4), CHUNK, axis=2),
                                 (2, SUB, 4 * CHUNK))
    bias_b = _rel_bias(relbias, tq, B_PREV) * LOG2E + jnp.asarray(
        np.where(_band_mask(tq, B_PREV, 8), 0.0, NEG)[None], F32)
    zero_state = jnp.zeros((batch, C_HEADS, LANES, LANES), F32)
    whole = lambda a: _const_spec(a.shape, (0,) * a.ndim)
    ret_consts = _ret_constants(tq)
    gla_consts = _gla_constants(tq)
    gain = _const_spec((None, 1, 512), (l, 0, 0))
    parts = [
        (_inproj_specs(l, d, tq, n_tiles, cur), _inproj_args(x, W, cos, sin)),
        ([whole(sink_rows)], [sink_rows]),
        ([whole(bias_b)], [bias_b]),
        ([whole(c) for c in ret_consts] + [gain, _state_spec()],
         [*ret_consts, W["gn_c"], zero_state]),
        ([whole(c) for c in gla_consts] + [gain, _state_spec()],
         [*gla_consts, W["gn_d"], zero_state]),
    ]
    bodies = (functools.partial(_attn_a_cols_kernel, masked_prefix=(A_PREV, tq)),
              functools.partial(_attn_b_kernel, n_kv=B_PREV // tq + 1, masked_prefix=(B_PREV, tq)),
              _ret_kernel,
              functools.partial(_gla_kernel, levels=_gla_levels(tq)))
    row_spec = lambda w: pl.BlockSpec((tq, w), lambda b, i: (cur(b, i), 0))
    state_shape = jax.ShapeDtypeStruct((batch, C_HEADS, LANES, LANES), F32)
    return pl.pallas_call(
        functools.partial(_front_kernel, n_in=tuple(len(p[0]) for p in parts), bodies=bodies,
                          prev_a=A_PREV, prev_b=B_PREV),
        grid=(batch, n_tiles),
        in_specs=[s for p in parts for s in p[0]],
        out_specs=[row_spec(2048), row_spec(256), row_spec(1024), _state_spec(), _state_spec()],
        out_shape=[jax.ShapeDtypeStruct((n, 2048), BF16), jax.ShapeDtypeStruct((n, 256), F32),
                   jax.ShapeDtypeStruct((n, 1024), F32), state_shape, state_shape],
        scratch_shapes=[pltpu.VMEM((tq, 512), F32), pltpu.VMEM((A_PREV + tq, 512), F32),
                        pltpu.VMEM((tq, 512), F32), pltpu.VMEM((B_PREV + tq, 1024), F32),
                        pltpu.VMEM((tq, 1536), F32), pltpu.VMEM((tq, 1792), F32),
                        pltpu.VMEM((C_HEADS, LANES, LANES), F32),
                        pltpu.VMEM((C_HEADS, LANES, LANES), F32)],
        compiler_params=_params(2, 60),
        name="front",
    )(*[a for p in parts for a in p[1]])


def _rope_tables(pos):
    half = HD // 2
    freqs = ROPE_THETA ** (-jnp.arange(half, dtype=F32) / half)
    ang = pos.astype(F32)[:, None] * freqs[None, :]
    cos = jnp.tile(jnp.cos(ang), (1, LANES // half))
    sin = jnp.sin(ang)
    sin = jnp.tile(jnp.concatenate([-sin, sin], axis=1), (1, LANES // HD))
    return cos, sin


def _dup_heads(w, col0, n_heads):
    lead = w.shape[:-1]
    blk = w[..., col0:col0 + n_heads * HD].reshape(*lead, n_heads, 1, HD)
    return jnp.broadcast_to(blk, (*lead, n_heads, 2, HD)).reshape(*lead, n_heads * 2 * HD)


def _prepare_weights(P):
    depth, d, _ = P["w_in"].shape
    w_in = P["w_in"]
    vec = lambda a: a.reshape(depth, 1, -1).astype(F32)
    W = {}
    W["wa"] = jnp.concatenate([w_in[..., 0:512], _dup_heads(w_in, 512, 2), _dup_heads(w_in, 640, 2)],
                              axis=-1).astype(BF16)
    W["wb"] = w_in[..., 768:2304].astype(BF16)
    W["wc"] = w_in[..., 2304:3840].astype(BF16)
    W["wd"] = jnp.pad(w_in[..., 3840:5392], ((0, 0), (0, 0), (0, LANES - GLA_RANK))).astype(BF16)
    W["wup"] = jnp.pad(P["w_alpha_up"], ((0, 0), (0, LANES - GLA_RANK), (0, 0))).astype(BF16)
    W["b_alpha"] = vec(P["b_alpha"])
    W["norm_mix"] = vec(P["norm_mix"])
    W["gqa"] = vec(jnp.tile(P["qn_a"], (1, 8)) * (HD ** -0.5 * LOG2E))
    W["gka"] = vec(jnp.tile(P["kn_a"], (1, 4)))
    W["gqb"] = vec(jnp.tile(P["qn_b"], (1, 8)) * (HD ** -0.5 * LOG2E))
    W["gkb"] = vec(jnp.tile(P["kn_b"], (1, 8)))
    W["gn_c"] = vec(P["gn_c"])
    W["gn_d"] = vec(P["gn_d"])
    W["sink"] = P["sink_a"].astype(F32) * LOG2E
    for name in ("w_merge", "w_branch", "w_out", "w_xq", "w_xkv", "w_xo", "w_up", "w_down"):
        W[name] = P[name].astype(BF16)
    for name in ("norm_x", "norm_mem", "qn_x", "kn_x", "norm_ffn"):
        W[name] = vec(P[name])
    return W


def _band_mask(q_rows, prev_rows, n_prev_chunks):
    r = np.arange(q_rows)[:, None] // CHUNK
    u = np.arange(prev_rows + q_rows)[None, :] // CHUNK - prev_rows // CHUNK
    return (u <= r) & (u >= r - n_prev_chunks)


def _rel_bias(table, q_rows, prev_rows):
    nk = prev_rows + q_rows
    period = q_rows + nk
    j = np.arange(period)
    offset = np.where(j < nk, j, j - period)
    idx = np.clip(prev_rows - offset, -REL_CLIP, REL_CLIP) + REL_CLIP
    v = table[:, idx]
    skew = jnp.tile(v, (1, q_rows))[:, :q_rows * (period - 1)]
    return skew.reshape(table.shape[0], q_rows, period - 1)[:, :, :nk]


def _mixers_sample(slabs, l, W, relbias, cache, batch, seq):
    za, zb, zc, zd = slabs
    win_k, win_v, band_k, band_v, s_c0, s_d0 = cache
    cur = lambda b, i: b
    la = win_k.shape[1]
    lb = band_k.shape[2]
    ck = jnp.broadcast_to(win_k[:, :, :, None, :], (batch, la, 2, 2, HD)).reshape(batch * la, 256)
    cv = jnp.broadcast_to(win_v[:, :, :, None, :], (batch, la, 2, 2, HD)).reshape(batch * la, 256)
    kv_a = [((ck, cv), la, 0, 0, 2, cur), ((za, za), seq, 512, 768, 2, cur)]
    part_a = _attention_operands(za, kv_a, _stacked_sinks(W["sink"][l], seq), seq, cur)
    body_a = functools.partial(_attn_a_kernel, n_kv=len(kv_a), chunk_rows=seq,
                               keys_per_chunk=la + seq, masked_prefix=None)

    band = (band_k.reshape(-1, 512), band_v.reshape(-1, 512))
    kv_b = [(band, lb, 0, 0, 4, lambda b, i: l * batch + b), ((zb, zb), seq, 512, 1024, 4, cur)]
    part_b = _attention_operands(zb, kv_b, _rel_bias(relbias, seq, lb) * LOG2E, seq, cur)
    body_b = functools.partial(_attn_b_kernel, n_kv=len(kv_b), masked_prefix=None)

    part_c = _recurrent_operands(zc, _ret_constants(seq), l, W["gn_c"], _state_to_kernel(s_c0),
                                 seq, 1)
    part_d = _recurrent_operands(zd, _gla_constants(seq), l, W["gn_d"], _state_to_kernel(s_d0),
                                 seq, 1)
    body_d = functools.partial(_gla_kernel, levels=_gla_levels(seq))
    return _mixers_call((part_a, part_b, part_c, part_d), (body_a, body_b, _ret_kernel, body_d),
                        (batch, 1), seq, cur, 40, "mixers_step")


def _tail_rows(kv_a, kv_b, batch, seq, rows_a, rows_b):
    ta = kv_a.reshape(batch, seq, -1)[:, seq - rows_a:]
    tb = kv_b.reshape(batch, seq, -1)[:, seq - rows_b:]
    ka = ta[:, :, 0:128].reshape(batch, rows_a, 2, HD)
    va = ta[:, :, 128:256].reshape(batch, rows_a, 2, HD)
    kb = tb[:, :, 0:512].reshape(batch, rows_b, 8, HD)
    vb = tb[:, :, 512:1024].reshape(batch, rows_b, 8, HD)
    return ka, va, kb, vb


def _slab_kv(slabs):
    za, zb = slabs[0], slabs[1]
    pick = lambda col0: jnp.concatenate([za[:, col0:col0 + HD], za[:, col0 + LANES:col0 + LANES + HD]], 1)
    return jnp.concatenate([pick(512), pick(768)], axis=1), zb[:, 512:1536]


def kernel(x_prompt, x_sample, cache_win_k, cache_win_v, cache_band_k, cache_band_v, state_ret, state_gla, cache_mem_k, cache_mem_v, mem_prompt, norm_mix, w_in, qn_a, kn_a, sink_a, qn_b, kn_b, relbias_b, gn_c, w_alpha_up, b_alpha, gn_d, w_branch, w_merge, w_out, norm_x, norm_mem, w_xq, w_xkv, qn_x, kn_x, w_xo, norm_ffn, w_up, w_down):
    P = dict(norm_mix=norm_mix, w_in=w_in, qn_a=qn_a, kn_a=kn_a, sink_a=sink_a, qn_b=qn_b,
             kn_b=kn_b, relbias_b=relbias_b, gn_c=gn_c, w_alpha_up=w_alpha_up, b_alpha=b_alpha,
             gn_d=gn_d, w_branch=w_branch, w_merge=w_merge, w_out=w_out, norm_x=norm_x,
             norm_mem=norm_mem, w_xq=w_xq, w_xkv=w_xkv, qn_x=qn_x, kn_x=kn_x, w_xo=w_xo,
             norm_ffn=norm_ffn, w_up=w_up, w_down=w_down)
    depth = w_in.shape[0]
    bp, tp, d = x_prompt.shape
    bs, ts, _ = x_sample.shape
    mlen = mem_prompt.shape[1]
    W = _prepare_weights(P)

    cos_p, sin_p = _rope_tables(jnp.arange(tp))
    cos_s, sin_s = _rope_tables(jnp.tile(PAST_LEN + jnp.arange(ts), bs))
    mem_k, mem_v = _memkv(mem_prompt.reshape(bp * mlen, d), W, depth)
    mem_k2 = mem_k.reshape(depth * bp * mlen, 512)
    mem_v2 = mem_v.reshape(depth * bp * mlen, 512)

    xp = x_prompt.reshape(bp * tp, d)
    xs = x_sample.reshape(bs * ts, d)
    tw = min(TOKEN_TILE, tp)
    n_wide = tp // tw
    rows_p, rows_s = [], []
    st_p, st_s = [], []
    for l in range(depth):
        relbias = relbias_b[l].astype(F32)
        outs, kv_a, kv_b, *states = _front(xp, l, W, cos_p, sin_p, relbias, bp, tp)
        rows_p.append(_tail_rows(kv_a, kv_b, bp, tp, min(A_PREV, tp), min(B_PREV, tp)))
        st_p.append(states)
        xp = _merge(xp, outs, l, W, tw)
        xp = _cross_ffn(xp, mem_k2, mem_v2, l, W, (bp, n_wide), tw,
                        lambda b, i: b * n_wide + i, 1, mlen, lambda b, i: l * bp + b)
        slabs = _inproj(xs, l, W, cos_s, sin_s, 1, bs * ts)
        cache = (cache_win_k[l], cache_win_v[l], cache_band_k, cache_band_v,
                 state_ret[l], state_gla[l])
        outs, *states = _mixers_sample(slabs, l, W, relbias, cache, bs, ts)
        rows_s.append(_tail_rows(*_slab_kv(slabs), bs, ts, ts, ts))
        st_s.append(states)
        xs = _merge(xs, outs, l, W, bs * ts)
        xs = _cross_ffn(xs, cache_mem_k.reshape(-1, 512), cache_mem_v.reshape(-1, 512), l, W,
                        (1, 1), bs * ts, lambda b, i: 0, bs, bs * mlen, lambda b, i: l)

    def stack_rows(rows, k):
        return jnp.stack([r[k] for r in rows])

    def stack_state(states, k):
        return jnp.stack([_state_from_kernel(s[k]) for s in states])

    return (xp.reshape(bp, tp, d), xs.reshape(bs, ts, d),
            stack_rows(rows_p, 0), stack_rows(rows_p, 1), stack_rows(rows_p, 2), stack_rows(rows_p, 3),
            stack_state(st_p, 0), stack_state(st_p, 1),
            mem_k.reshape(depth, bp, mlen, 4, LANES), mem_v.reshape(depth, bp, mlen, 4, LANES),
            stack_rows(rows_s, 0), stack_rows(rows_s, 1), stack_rows(rows_s, 2), stack_rows(rows_s, 3),
            stack_state(st_s, 0), stack_state(st_s, 1))
```

```python
import functools

import numpy as np
import jax
import jax.numpy as jnp
from jax import lax
from jax.experimental import pallas as pl
from jax.experimental.pallas import tpu as pltpu

F32 = jnp.float32
BF16 = jnp.bfloat16

EPS = 1e-6
NEG = -1e30
LOG2E = 1.4426950408889634
PAST_LEN = 1024
CHUNK = 64
ROPE_THETA = 10000.0
A_PREV = 2 * CHUNK
B_PREV = 8 * CHUNK
REL_CLIP = 128
GLA_RANK = 16
GLA_TAU = 16.0
C_HEADS = 4
LANES = 128
SUB = 8
HD = 64
SEQ_TILE = 256
TOKEN_TILE = 512
MIB = 2 ** 20


def _dot(a, b):
    return jnp.dot(a.astype(BF16), b.astype(BF16), preferred_element_type=F32)


def _dot_nt(a, b):
    return lax.dot_general(a.astype(BF16), b.astype(BF16), (((1,), (1,)), ((), ())),
                           preferred_element_type=F32)


def _dot_tn(a, b):
    return lax.dot_general(a.astype(BF16), b.astype(BF16), (((0,), (0,)), ((), ())),
                           preferred_element_type=F32)


def _split3(x):
    x1 = x.astype(BF16)
    r1 = x - x1.astype(F32)
    x2 = r1.astype(BF16)
    x3 = (r1 - x2.astype(F32)).astype(BF16)
    return x1, x2, x3


def _dot_01_left(m01, x):
    x1, x2, x3 = _split3(x)
    dot = functools.partial(jnp.dot, preferred_element_type=F32)
    return dot(m01, x1) + dot(m01, x2) + dot(m01, x3)


def _rms_rows(x, g):
    return x * lax.rsqrt(jnp.mean(x * x, axis=-1, keepdims=True) + EPS) * g


def _lane_iota(shape):
    return lax.broadcasted_iota(jnp.int32, shape, len(shape) - 1)


def _group_ones(width):
    r = lax.broadcasted_iota(jnp.int32, (width, width), 0) // HD
    c = lax.broadcasted_iota(jnp.int32, (width, width), 1) // HD
    return jnp.where(r == c, 1.0, 0.0).astype(BF16)


def _head64_norm(x, g, ones):
    sq = x * x
    hi = sq.astype(BF16)
    lo = (sq - hi.astype(F32)).astype(BF16)
    ss = (jnp.dot(hi, ones, preferred_element_type=F32)
          + jnp.dot(lo, ones, preferred_element_type=F32))
    return x * lax.rsqrt(ss * (1.0 / HD) + EPS) * g


def _rope64(x, cos, sin):
    first_half = (_lane_iota((1, LANES)) % HD) < (HD // 2)
    swapped = jnp.where(first_half, pltpu.roll(x, LANES - HD // 2, 1), pltpu.roll(x, HD // 2, 1))
    return x * cos + swapped * sin


def _sigmoid(x):
    return 1.0 / (1.0 + jnp.exp(-x))


def _log_sigmoid(x):
    return jnp.minimum(x, 0.0) - jnp.log(1.0 + jnp.exp(-jnp.abs(x)))


def _params(n_axes, vmem_mib):
    return pltpu.CompilerParams(dimension_semantics=("arbitrary",) * n_axes,
                                vmem_limit_bytes=vmem_mib * MIB)


def _const_spec(shape, index):
    return pl.BlockSpec(shape, lambda *_: index, pipeline_mode=pl.Buffered(1))


def _inproj_stages(x_ref, g_ref, wa_ref, wb_ref, wc_ref, wd_ref, wup_ref, ba_ref,
                   gqa_ref, gka_ref, gqb_ref, gkb_ref, cos_ref, sin_ref, dst):
    h = _rms_rows(x_ref[...], g_ref[...]).astype(BF16)
    cos = cos_ref[...]
    sin = sin_ref[...]
    pair = 2 * LANES
    ones = _group_ones(pair)

    za = jnp.dot(h, wa_ref[...], preferred_element_type=F32)
    zb = jnp.dot(h, wb_ref[...], preferred_element_type=F32)
    yield
    for p in range(3):
        sl = slice(p * pair, (p + 1) * pair)
        gain = gqa_ref[:, sl] if p < 2 else gka_ref[...]
        y = _head64_norm(za[:, sl], gain, ones)
        ref, col0 = (dst["a_q"], p * pair) if p < 2 else (dst["a_k"], 0)
        for c in range(2):
            ref[:, col0 + c * LANES:col0 + (c + 1) * LANES] = _rope64(
                y[:, c * LANES:(c + 1) * LANES], cos, sin)
    dst["a_v"][...] = za[:, 768:1024]
    zc = jnp.dot(h, wc_ref[...], preferred_element_type=F32)
    yield
    for p in range(4):
        sl = slice(p * pair, (p + 1) * pair)
        if p < 2:
            dst["b_q"][:, sl] = _head64_norm(zb[:, sl], gqb_ref[:, sl], ones)
        else:
            ksl = slice((p - 2) * pair, (p - 1) * pair)
            dst["b_k"][:, ksl] = _head64_norm(zb[:, sl], gkb_ref[:, ksl], ones)
    dst["b_v"][...] = zb[:, 1024:1536]
    zd = jnp.dot(h, wd_ref[...], preferred_element_type=F32)
    yield
    oc_ref = dst["c"]
    for c in range(2):
        sl = slice(c * LANES, (c + 1) * LANES)
        oc_ref[:, sl] = _rope64(zc[:, sl], cos, sin)
    for c in range(2):
        sl = slice(256 + c * LANES, 256 + (c + 1) * LANES)
        oc_ref[:, sl] = _rope64(zc[:, sl], cos, sin) * (HD ** -0.5)
    oc_ref[:, 512:1536] = zc[:, 512:1536]
    yield
    od_ref = dst["d"]
    od_ref[:, 0:256] = zd[:, 0:256] * (HD ** -0.5)
    od_ref[:, 256:1536] = zd[:, 256:1536]
    pre = _dot(zd[:, 1536:1664], wup_ref[...]) + ba_ref[...]
    od_ref[:, 1536:1792] = _log_sigmoid(pre) * (LOG2E / GLA_TAU)


def _inproj_kernel(*refs):
    *ins, oa_ref, ob_ref, oc_ref, od_ref = refs
    dst = dict(a_q=oa_ref.at[:, 0:512], a_k=oa_ref.at[:, 512:768], a_v=oa_ref.at[:, 768:1024],
               b_q=ob_ref.at[:, 0:512], b_k=ob_ref.at[:, 512:1024], b_v=ob_ref.at[:, 1024:1536],
               c=oc_ref, d=od_ref)
    for _ in _inproj_stages(*ins, dst):
        pass


def _inproj_specs(l, d, tm, n_pos_blocks, row_index):
    lay = (l, 0, 0)
    return [
        pl.BlockSpec((tm, d), lambda *g: (row_index(*g), 0)),
        _const_spec((None, 1, d), lay),
        _const_spec((None, d, 1024), lay),
        _const_spec((None, d, 1536), lay),
        _const_spec((None, d, 1536), lay),
        _const_spec((None, d, 1664), lay),
        _const_spec((None, LANES, 256), lay),
        _const_spec((None, 1, 256), lay),
        _const_spec((None, 1, 512), lay),
        _const_spec((None, 1, 256), lay),
        _const_spec((None, 1, 512), lay),
        _const_spec((None, 1, 512), lay),
        pl.BlockSpec((tm, LANES), lambda *g: (row_index(*g) % n_pos_blocks, 0)),
        pl.BlockSpec((tm, LANES), lambda *g: (row_index(*g) % n_pos_blocks, 0)),
    ]


def _inproj_args(x, W, cos, sin):
    return [x, W["norm_mix"], W["wa"], W["wb"], W["wc"], W["wd"], W["wup"], W["b_alpha"],
            W["gqa"], W["gka"], W["gqb"], W["gkb"], cos, sin]


def _inproj(x, l, W, cos, sin, n_pos_blocks, tm):
    n, d = x.shape
    row = lambda i: (i, 0)
    widths = (1024, 1536, 1536, 1792)
    return pl.pallas_call(
        _inproj_kernel,
        grid=(n // tm,),
        in_specs=_inproj_specs(l, d, tm, n_pos_blocks, lambda i: i),
        out_specs=[pl.BlockSpec((tm, w), row) for w in widths],
        out_shape=[jax.ShapeDtypeStruct((n, w), F32) for w in widths],
        compiler_params=_params(1, 56),
        name="inproj",
    )(*_inproj_args(x, W, cos, sin))


def _stack_rows(refs):
    blocks = [r[...].astype(BF16) for r in refs]
    return blocks[0] if len(blocks) == 1 else jnp.concatenate(blocks, axis=0)


def _first_valid_key(masked_prefix):
    prev_rows, seq_tile = masked_prefix
    return prev_rows - pl.program_id(1) * seq_tile


def _attn_a_kernel(*refs, n_kv, chunk_rows, keys_per_chunk, masked_prefix):
    q_ref = refs[0]
    k_all = _stack_rows(refs[1:1 + n_kv])
    v_all = _stack_rows(refs[1 + n_kv:1 + 2 * n_kv])
    sink_ref = refs[1 + 2 * n_kv]
    o_ref = refs[2 + 2 * n_kv]
    cr = chunk_rows
    low = _lane_iota((1, LANES)) < HD
    units = [(j, g) for j in range(q_ref.shape[0] // cr) for g in range(2)]
    keys_of = lambda j: slice(j * cr, j * cr + keys_per_chunk)
    group = lambda g: slice(g * LANES, (g + 1) * LANES)
    scores = []
    for j, g in units:
        stacked = []
        for c in (2 * g, 2 * g + 1):
            q_c = q_ref[j * cr:(j + 1) * cr, c * LANES:(c + 1) * LANES]
            stacked += [jnp.where(low, q_c, 0.0), jnp.where(low, 0.0, q_c)]
        s = _dot_nt(jnp.concatenate(stacked, axis=0), k_all[keys_of(j), group(g)])
        if masked_prefix is not None:
            key_ok = _lane_iota((1, keys_per_chunk)) >= _first_valid_key(masked_prefix) - j * cr
            s = jnp.where(key_ok, s, NEG)
        scores.append(s)
    yield
    probs = []
    for (j, g), s in zip(units, scores):
        sink = sink_ref[g][:, 0:1]
        m = jnp.maximum(jnp.max(s, axis=-1, keepdims=True), sink)
        p = jnp.exp2(s - m)
        probs.append((p.astype(BF16), jnp.sum(p, axis=-1, keepdims=True) + jnp.exp2(sink - m)))
    yield
    outs = [_dot(p, v_all[keys_of(j), group(g)]) / den for (j, g), (p, den) in zip(units, probs)]
    yield
    for (j, g), o in zip(units, outs):
        for t, c in enumerate((2 * g, 2 * g + 1)):
            o_ref[j * cr:(j + 1) * cr, c * LANES:(c + 1) * LANES] = jnp.where(
                low, o[2 * t * cr:(2 * t + 1) * cr],
                o[(2 * t + 1) * cr:(2 * t + 2) * cr]).astype(o_ref.dtype)


def _attn_a_cols_kernel(q_ref, kp_ref, kc_ref, vp_ref, vc_ref, sink_ref, o_ref, *, masked_prefix):
    prev_rows = kp_ref.shape[0]
    k_all = _stack_rows((kp_ref, kc_ref))
    v_all = _stack_rows((vp_ref, vc_ref))
    cr = CHUNK
    keys_per_chunk = prev_rows + cr
    lane_low = _lane_iota((1, LANES)) < HD
    same_half = ((lax.broadcasted_iota(jnp.int32, (LANES, LANES), 0) < HD)
                 == (_lane_iota((LANES, LANES)) < HD))
    key_row = lax.broadcasted_iota(jnp.int32, (keys_per_chunk, 2 * LANES), 0)
    units = [(j, g) for j in range(q_ref.shape[0] // cr) for g in range(2)]
    keys_of = lambda j: slice(j * cr, j * cr + keys_per_chunk)
    group = lambda g: slice(g * LANES, (g + 1) * LANES)

    scores = []
    for j, g in units:
        q_t = []
        for c in (2 * g, 2 * g + 1):
            q_c = q_ref[j * cr:(j + 1) * cr, c * LANES:(c + 1) * LANES]
            both = jnp.concatenate([q_c, q_c], axis=0).T
            q_t.append(jnp.where(same_half, both, 0.0).astype(BF16))
        s_t = jnp.dot(k_all[keys_of(j), group(g)], jnp.concatenate(q_t, axis=1),
                      preferred_element_type=F32)
        if j * cr < prev_rows:
            s_t = jnp.where(key_row >= _first_valid_key(masked_prefix) - j * cr, s_t, NEG)
        scores.append(s_t)
    yield
    probs = []
    for (j, g), s_t in zip(units, scores):
        sink = sink_ref[g][0:1, :]
        m = jnp.maximum(jnp.max(s_t, axis=0, keepdims=True), sink)
        p_t = jnp.exp2(s_t - m)
        den = jnp.sum(p_t, axis=0, keepdims=True) + jnp.exp2(sink - m)
        probs.append((p_t.astype(BF16), den))
    yield
    outs = []
    for (j, g), (p_t, den) in zip(units, probs):
        outs.append(_dot_tn(v_all[keys_of(j), group(g)], p_t) / den)
    yield
    for (j, g), o_t in zip(units, outs):
        for t, c in enumerate((2 * g, 2 * g + 1)):
            blk = o_t[:, t * LANES:(t + 1) * LANES].T
            o_ref[j * cr:(j + 1) * cr, c * LANES:(c + 1) * LANES] = jnp.where(
                lane_low, blk[0:cr], blk[cr:2 * cr]).astype(o_ref.dtype)


def _attn_b_kernel(*refs, n_kv, masked_prefix):
    q_ref = refs[0]
    k_all = _stack_rows(refs[1:1 + n_kv])
    v_all = _stack_rows(refs[1 + n_kv:1 + 2 * n_kv])
    bias_ref = refs[1 + 2 * n_kv]
    o_ref = refs[2 + 2 * n_kv]
    low = _lane_iota((1, LANES)) < HD
    if masked_prefix is not None:
        key_ok = _lane_iota((1, k_all.shape[0])) >= _first_valid_key(masked_prefix)
    slab = lambda c: slice(c * LANES, (c + 1) * LANES)
    heads = [(c, half) for c in range(4) for half in range(2)]
    scores = []
    for c, half in heads:
        q_c = q_ref[:, slab(c)]
        q_h = jnp.where(low, q_c, 0.0) if half == 0 else jnp.where(low, 0.0, q_c)
        scores.append(_dot_nt(q_h, k_all[:, slab(c)]))
    yield
    probs = []
    for (c, half), s in zip(heads, scores):
        s = s + bias_ref[2 * c + half]
        if masked_prefix is not None:
            s = jnp.where(key_ok, s, NEG)
        p = jnp.exp2(s - jnp.max(s, axis=-1, keepdims=True))
        probs.append((p.astype(BF16), jnp.sum(p, axis=-1, keepdims=True)))
    yield
    outs = [_dot(p, v_all[:, slab(c)]) / den for (c, half), (p, den) in zip(heads, probs)]
    yield
    for c in range(4):
        o_ref[:, slab(c)] = jnp.where(low, outs[2 * c], outs[2 * c + 1]).astype(o_ref.dtype)


def _attn_b_cols_kernel(*refs, n_kv, masked_prefix):
    q_ref = refs[0]
    k_all = _stack_rows(refs[1:1 + n_kv])
    v_refs = refs[1 + n_kv:1 + 2 * n_kv]
    bias_ref = refs[1 + 2 * n_kv]
    o_ref = refs[2 + 2 * n_kv]
    tq = q_ref.shape[0]
    nk = k_all.shape[0]
    slab = lambda c: slice(c * LANES, (c + 1) * LANES)
    heads = [(c, half) for c in range(4) for half in range(2)]
    top = lax.broadcasted_iota(jnp.int32, (LANES, tq), 0) < HD
    key_ok = lax.broadcasted_iota(jnp.int32, (nk, tq), 0) >= _first_valid_key(masked_prefix)
    scores = []
    for c in range(4):
        q_t = q_ref[:, slab(c)].T
        k_c = k_all[:, slab(c)]
        for half in range(2):
            q_h = (jnp.where(top, q_t, 0.0) if half == 0 else jnp.where(top, 0.0, q_t)).astype(BF16)
            scores.append(jnp.dot(k_c, q_h, preferred_element_type=F32))
    yield
    probs = []
    for (c, half), s in zip(heads, scores):
        s = jnp.where(key_ok, s + bias_ref[2 * c + half], NEG)
        p = jnp.exp2(s - jnp.max(s, axis=0, keepdims=True))
        probs.append((p.astype(BF16), jnp.sum(p, axis=0, keepdims=True)))
    yield
    yield
    outs = []
    for c in range(4):
        v_c = jnp.concatenate([r[:, slab(c)] for r in v_refs], axis=0)
        v_t = v_c.T.astype(BF16)
        for half in range(2):
            p, den = probs[2 * c + half]
            outs.append(jnp.dot(v_t, p, preferred_element_type=F32) / den)
    yield
    for c in range(4):
        both = jnp.where(top, outs[2 * c], outs[2 * c + 1])
        o_ref[:, slab(c)] = both.T.astype(o_ref.dtype)


def _attention_operands(q_arr, kv_arrs, extra, q_rows, q_index):
    in_specs = [pl.BlockSpec((q_rows, 512), lambda b, i: (q_index(b, i), 0))]
    args = [q_arr]
    for use_v in (False, True):
        for arrs, rows, kc0, vc0, ncol, index_fn in kv_arrs:
            width = ncol * LANES
            col_block = (vc0 if use_v else kc0) // width
            in_specs.append(pl.BlockSpec(
                (rows, width),
                functools.partial(lambda b, i, f, cb: (f(b, i), cb), f=index_fn, cb=col_block)))
            args.append(arrs[1] if use_v else arrs[0])
    in_specs.append(_const_spec(extra.shape, (0,) * extra.ndim))
    args.append(extra)
    return in_specs, args


def _stacked_sinks(sink, chunk_rows):
    col = jnp.repeat(sink.reshape(2, 4), chunk_rows, axis=1)
    return jnp.broadcast_to(col[:, :, None], (2, 4 * chunk_rows, LANES)).astype(F32)


def _head_lane_mask(head):
    low = _lane_iota((1, LANES)) < HD
    return low if head % 2 == 0 else jnp.logical_not(low)


def _ret_kernel(z_ref, dmat_ref, qdec_ref, kdec_ref, sdec_ref, gn_ref, o_ref, st_ref):
    q = z_ref[:, 0:256]
    k = z_ref[:, 256:512]
    q_in = q * qdec_ref[...]
    k_st = k * kdec_ref[...]
    heads = range(C_HEADS)
    pair = lambda h: slice((h // 2) * LANES, (h // 2 + 1) * LANES)
    v = [z_ref[:, 512 + h * LANES:512 + (h + 1) * LANES].astype(BF16) for h in heads]
    old_state = [st_ref[h] for h in heads]
    raw = [_dot_nt(q[:, pair(h)], jnp.where(_head_lane_mask(h), k[:, pair(h)], 0.0)) for h in heads]
    inter = [_dot_nt(q_in[:, pair(h)], old_state[h]) for h in heads]
    state_upd = [_dot_tn(v[h], jnp.where(_head_lane_mask(h), k_st[:, pair(h)], 0.0)) for h in heads]
    yield
    scores = [(raw[h] * dmat_ref[h]).astype(BF16) for h in heads]
    yield
    outs = [jnp.dot(scores[h], v[h], preferred_element_type=F32) + inter[h] for h in heads]
    yield
    for h in heads:
        o = outs[h]
        g_h = z_ref[:, 1024 + h * LANES:1024 + (h + 1) * LANES]
        oc = o - jnp.mean(o, axis=-1, keepdims=True)
        y = oc * lax.rsqrt(jnp.mean(oc * oc, axis=-1, keepdims=True) + EPS)
        y = y * gn_ref[:, h * LANES:(h + 1) * LANES]
        o_ref[:, h * LANES:(h + 1) * LANES] = (y * (g_h * _sigmoid(g_h))).astype(o_ref.dtype)
        st_ref[h] = old_state[h] * sdec_ref[:, pair(h)] + state_upd[h]


def _gla_kernel(z_ref, tril_ref, lmask_ref, bd_ref, ind_ref, rep_ref, gn_ref, o_ref, st_ref, *,
                levels):
    tc = z_ref.shape[0]
    diag = bd_ref.shape[0]
    q = z_ref[:, 0:256]
    k = z_ref[:, 256:512]
    la = z_ref[:, 1536:1792]
    cum = _dot_01_left(tril_ref[...], la)
    cum_last = cum[tc - 1:tc, :]
    yield

    off_diag = []
    for size in (s for s in levels if s > diag):
        for start in range(0, tc, size):
            m = start + size // 2 - 1
            rows = slice(start + size // 2, start + size)
            cols = slice(start, start + size // 2)
            q_r = q[rows] * jnp.exp2(cum[rows] - cum[m:m + 1])
            k_c = k[cols] * jnp.exp2(cum[m:m + 1] - cum[cols])
            off_diag.append((rows, cols, q_r, k_c))
    small = [s for s in levels if s <= diag]
    q_lv, k_lv = [], []
    for size in small:
        mids = []
        for start in range(0, tc, size):
            m = start + size // 2 - 1
            mids.append(jnp.broadcast_to(cum[m:m + 1, :], (size, 256)))
        cm = mids[0] if len(mids) == 1 else jnp.concatenate(mids, axis=0)
        e = jnp.exp2(-jnp.abs(cum - cm))
        q_lv.append(q * e)
        k_lv.append(k * e)

    nb = tc // SUB
    q3 = q.reshape(nb, SUB, 256)
    k3 = k.reshape(nb, SUB, 256)
    c3 = cum.reshape(nb, SUB, 256)
    row_in_block = lax.broadcasted_iota(jnp.int32, (1, SUB, 1), 1)
    r = None
    for j in range(SUB):
        e = jnp.exp2(jnp.where(row_in_block >= j, c3 - c3[:, j:j + 1, :], NEG))
        t = q3 * k3[:, j:j + 1, :] * e
        part = jnp.dot(t.reshape(tc, 256).astype(BF16), ind_ref[j], preferred_element_type=F32)
        r = part if r is None else r + part
    r = r.astype(BF16)
    bd = bd_ref[...]
    yield

    q_in = q * jnp.exp2(cum)
    k_st = k * jnp.exp2(cum_last - cum)
    s_dec = jnp.exp2(cum_last)

    heads = range(C_HEADS)
    pair = lambda h: slice((h // 2) * LANES, (h // 2 + 1) * LANES)
    block = lambda b: slice(b * diag, (b + 1) * diag)
    n_blocks = tc // diag
    v = [z_ref[:, 512 + h * LANES:512 + (h + 1) * LANES].astype(BF16) for h in heads]
    old_state = [st_ref[h] for h in heads]
    inter = [_dot_nt(q_in[:, pair(h)], old_state[h]) for h in heads]
    state_upd = [_dot_tn(v[h], jnp.where(_head_lane_mask(h), k_st[:, pair(h)], 0.0)) for h in heads]
    raw = {}
    for h in heads:
        hm = _head_lane_mask(h)
        for b in range(n_blocks):
            raw[h, b, "blk8"] = jnp.dot(r[block(b)], rep_ref[h], preferred_element_type=F32)
            for lv in range(len(small)):
                k_h = jnp.where(hm, k_lv[lv][block(b), pair(h)], 0.0)
                raw[h, b, lv] = _dot_nt(q_lv[lv][block(b), pair(h)], k_h)
        for n, (rows, cols, q_r, k_c) in enumerate(off_diag):
            raw[h, "off", n] = _dot_nt(q_r[:, pair(h)], jnp.where(hm, k_c[:, pair(h)], 0.0))
    yield
    scores = {}
    for h in heads:
        for b in range(n_blocks):
            s_b = raw[h, b, "blk8"] * bd
            for lv in range(len(small)):
                s_b = s_b + raw[h, b, lv] * lmask_ref[lv]
            scores[h, b] = s_b.astype(BF16)
    yield
    outs = []
    for h in heads:
        blocks = [jnp.dot(scores[h, b], v[h][block(b)], preferred_element_type=F32)
                  for b in range(n_blocks)]
        for n, (rows, cols, _, _) in enumerate(off_diag):
            o_rc = _dot(raw[h, "off", n], v[h][cols])
            for b in range(rows.start // diag, rows.stop // diag):
                lo = b * diag - rows.start
                blocks[b] = blocks[b] + o_rc[lo:lo + diag]
        o = blocks[0] if n_blocks == 1 else jnp.concatenate(blocks, axis=0)
        outs.append(o + inter[h])
    yield
    for h in heads:
        o = outs[h]
        g_h = z_ref[:, 1024 + h * LANES:1024 + (h + 1) * LANES]
        y = o * lax.rsqrt(jnp.mean(o * o, axis=-1, keepdims=True) + EPS)
        y = y * gn_ref[:, h * LANES:(h + 1) * LANES]
        o_ref[:, h * LANES:(h + 1) * LANES] = (y * (g_h * _sigmoid(g_h))).astype(o_ref.dtype)
        st_ref[h] = old_state[h] * s_dec[:, pair(h)] + state_upd[h]


def _ret_constants(tc):
    log_gamma = np.log1p(-np.exp2(-5.0 - np.arange(C_HEADS, dtype=np.float64)))
    idx = np.arange(tc)
    diff = idx[:, None] - idx[None, :]
    dmat = np.where(diff >= 0, np.exp(log_gamma[:, None, None] * np.maximum(diff, 0)[None]), 0.0)
    per_lane = np.repeat(log_gamma, HD)
    qdec = np.exp((idx[:, None] + 1) * per_lane[None, :])
    kdec = np.exp((tc - 1 - idx[:, None]) * per_lane[None, :])
    sdec = np.exp(tc * per_lane)[None, :]
    return [jnp.asarray(a, F32) for a in (dmat, qdec, kdec, sdec)]


def _gla_levels(tc):
    levels = []
    size = tc
    while size >= 2 * SUB:
        levels.append(size)
        size //= 2
    return tuple(levels)


def _gla_diag(tc):
    return min(tc, LANES)


def _gla_constants(tc):
    diag = _gla_diag(tc)
    idx = np.arange(tc)
    tril = (idx[None, :] <= idx[:, None]).astype(np.float32)
    idx = np.arange(diag)
    i, j = idx[:, None], idx[None, :]
    lmasks = []
    for size in (s for s in _gla_levels(tc) if s <= diag):
        lmasks.append(((i // size == j // size) & (i % size >= size // 2) & (j % size < size // 2)))
    lmask = np.stack(lmasks).astype(np.float32)
    bd = (i // SUB == j // SUB).astype(np.float32)
    ind = np.zeros((SUB, 256, LANES), np.float32)
    for jj in range(SUB):
        for h in range(C_HEADS):
            ind[jj, h * HD:(h + 1) * HD, h * SUB + jj] = 1.0
    rep = np.zeros((C_HEADS, LANES, diag), np.float32)
    for h in range(C_HEADS):
        for jj in range(SUB):
            rep[h, h * SUB + jj, jj::SUB] = 1.0
    return (jnp.asarray(tril, BF16), jnp.asarray(lmask, F32), jnp.asarray(bd, F32),
            jnp.asarray(ind, BF16), jnp.asarray(rep, BF16))


def _state_spec():
    return pl.BlockSpec((None, C_HEADS, LANES, LANES), lambda b, i: (b, 0, 0, 0))


def _recurrent_operands(z, consts, l, gn, s0, tc, n_tiles):
    in_specs = [pl.BlockSpec((tc, z.shape[1]), lambda b, i: (b * n_tiles + i, 0))]
    in_specs += [_const_spec(c.shape, (0,) * c.ndim) for c in consts]
    in_specs += [_const_spec((None, 1, 512), (l, 0, 0)), _state_spec()]
    return in_specs, [z, *consts, gn, s0]


def _mixers_kernel(*refs, bodies, n_in):
    ends = np.cumsum(n_in)
    a_in, b_in, c_in, d_in = (refs[e - n:e] for e, n in zip(ends, n_in))
    o_ref, sc_ref, sd_ref, stc_ref, std_ref = refs[ends[-1]:]
    *c_in, s0c_ref = c_in
    *d_in, s0d_ref = d_in
    i = pl.program_id(1)

    @pl.when(i == 0)
    def _():
        stc_ref[...] = s0c_ref[...]
        std_ref[...] = s0d_ref[...]

    body_a, body_b, body_c, body_d = bodies
    running = [body_d(*d_in, o_ref.at[:, 1536:2048], std_ref),
               body_b(*b_in, o_ref.at[:, 512:1024]),
               body_a(*a_in, o_ref.at[:, 0:512]),
               body_c(*c_in, o_ref.at[:, 1024:1536], stc_ref)]
    while running:
        for gen in list(running):
            if next(gen, "done") == "done":
                running.remove(gen)

    @pl.when(i == pl.num_programs(1) - 1)
    def _():
        sc_ref[...] = stc_ref[...]
        sd_ref[...] = std_ref[...]


def _mixers_call(parts, bodies, grid, rows, row_index, vmem_mib, name):
    in_specs = [s for specs, _ in parts for s in specs]
    args = [a for _, arrs in parts for a in arrs]
    n_in = tuple(len(specs) for specs, _ in parts)
    n_rows = grid[0] * grid[1] * rows
    state_shape = jax.ShapeDtypeStruct((grid[0], C_HEADS, LANES, LANES), F32)
    return pl.pallas_call(
        functools.partial(_mixers_kernel, bodies=bodies, n_in=n_in),
        grid=grid,
        in_specs=in_specs,
        out_specs=[pl.BlockSpec((rows, 2048), lambda b, i: (row_index(b, i), 0)),
                   _state_spec(), _state_spec()],
        out_shape=[jax.ShapeDtypeStruct((n_rows, 2048), BF16), state_shape, state_shape],
        scratch_shapes=[pltpu.VMEM((C_HEADS, LANES, LANES), F32)] * 2,
        compiler_params=_params(2, vmem_mib),
        name=name,
    )(*args)


def _state_to_kernel(s):
    st = jnp.swapaxes(s, -1, -2)
    lo = jnp.pad(st, ((0, 0), (0, 0), (0, 0), (0, HD)))
    hi = jnp.pad(st, ((0, 0), (0, 0), (0, 0), (HD, 0)))
    odd = (jnp.arange(C_HEADS) % 2 == 1)[None, :, None, None]
    return jnp.where(odd, hi, lo)


def _state_from_kernel(st):
    lo = st[..., :HD]
    hi = st[..., HD:]
    odd = (jnp.arange(C_HEADS) % 2 == 1)[None, :, None, None]
    return jnp.swapaxes(jnp.where(odd, hi, lo), -1, -2)


def _merge_kernel(x_ref, g_ref, o_ref, wm_ref, wbr_ref, wo_ref, y_ref):
    x = x_ref[...]
    d = x.shape[1]
    h = _rms_rows(x, g_ref[...]).astype(BF16)
    acc = None
    for b in range(4):
        gate = _sigmoid(jnp.dot(h, wm_ref[:, b * d:(b + 1) * d], preferred_element_type=F32))
        term = gate * jnp.dot(o_ref[:, b * 512:(b + 1) * 512], wbr_ref[b],
                              preferred_element_type=F32)
        acc = term if acc is None else acc + term
    y_ref[...] = x + _dot(acc, wo_ref[...])


def _merge(x, outs, l, W, tm):
    n, d = x.shape
    row = lambda i: (i, 0)
    lay = (l, 0, 0)
    in_specs = [pl.BlockSpec((tm, d), row), _const_spec((None, 1, d), lay),
                pl.BlockSpec((tm, 2048), row),
                _const_spec((None, d, 4 * d), lay),
                _const_spec((None, 4, 512, d), (l, 0, 0, 0)),
                _const_spec((None, d, d), lay)]
    return pl.pallas_call(
        _merge_kernel,
        grid=(n // tm,),
        in_specs=in_specs,
        out_specs=pl.BlockSpec((tm, d), row),
        out_shape=jax.ShapeDtypeStruct((n, d), F32),
        compiler_params=_params(1, 48),
        name="merge",
    )(x, W["norm_mix"], outs, W["w_merge"], W["w_branch"], W["w_out"])


def _cross_ffn_kernel(x_ref, gx_ref, wq_ref, qn_ref, mk_ref, mv_ref, wxo_ref, gf_ref, wup_ref,
                      wdn_ref, y_ref, *, n_seq):
    x = x_ref[...]
    d = x.shape[1]
    rows = x.shape[0] // n_seq
    mlen = mk_ref.shape[0] // n_seq
    hx = _rms_rows(x, gx_ref[...])
    q = _dot(hx, wq_ref[...])
    slab = lambda h: slice(h * LANES, (h + 1) * LANES)
    units = [(h, b) for h in range(4) for b in range(n_seq)]
    mem_of = lambda b: slice(b * mlen, (b + 1) * mlen)
    q_n = [(_rms_rows(q[:, slab(h)], qn_ref[...]) * (LANES ** -0.5 * LOG2E)).astype(BF16)
           for h in range(4)]
    scores = [_dot_nt(q_n[h][b * rows:(b + 1) * rows], mk_ref[mem_of(b), slab(h)])
              for h, b in units]
    probs = []
    for s in scores:
        p = jnp.exp2(s - jnp.max(s, axis=-1, keepdims=True))
        probs.append((p.astype(BF16), jnp.sum(p, axis=-1, keepdims=True)))
    outs = [_dot(p, mv_ref[mem_of(b), slab(h)]) / den for (h, b), (p, den) in zip(units, probs)]
    heads = []
    for h in range(4):
        per_seq = outs[h * n_seq:(h + 1) * n_seq]
        heads.append(per_seq[0] if n_seq == 1 else jnp.concatenate(per_seq, axis=0))
    x = x + _dot(jnp.concatenate(heads, axis=1), wxo_ref[...])
    hf = _rms_rows(x, gf_ref[...]).astype(BF16)
    acc = None
    for j in range(wup_ref.shape[1] // d):
        sl = slice(j * d, (j + 1) * d)
        u = jnp.maximum(jnp.dot(hf, wup_ref[:, sl], preferred_element_type=F32), 0.0)
        term = _dot(u * u, wdn_ref[sl, :])
        acc = term if acc is None else acc + term
    y_ref[...] = x + acc


def _cross_ffn(x, mk, mv, l, W, grid, rows, x_index, n_seq, mem_rows, mem_index):
    n, d = x.shape
    lay = (l, 0, 0)
    dff = W["w_up"].shape[-1]
    in_specs = [pl.BlockSpec((rows, d), lambda b, i: (x_index(b, i), 0)),
                _const_spec((None, 1, d), lay),
                _const_spec((None, d, 512), lay),
                _const_spec((None, 1, LANES), lay),
                pl.BlockSpec((mem_rows, 512), lambda b, i: (mem_index(b, i), 0)),
                pl.BlockSpec((mem_rows, 512), lambda b, i: (mem_index(b, i), 0)),
                _const_spec((None, 512, d), lay),
                _const_spec((None, 1, d), lay),
                _const_spec((None, d, dff), lay),
                _const_spec((None, dff, d), lay)]
    return pl.pallas_call(
        functools.partial(_cross_ffn_kernel, n_seq=n_seq),
        grid=grid,
        in_specs=in_specs,
        out_specs=pl.BlockSpec((rows, d), lambda b, i: (x_index(b, i), 0)),
        out_shape=jax.ShapeDtypeStruct((n, d), F32),
        compiler_params=_params(2, 48),
        name="cross_ffn",
    )(x, W["norm_x"], W["w_xq"], W["qn_x"], mk, mv, W["w_xo"], W["norm_ffn"], W["w_up"],
      W["w_down"])


def _memkv_kernel(mem_ref, g_ref, w_ref, kn_ref, k_ref, v_ref):
    hm = _rms_rows(mem_ref[...], g_ref[...])
    kv = _dot(hm, w_ref[...])
    for h in range(4):
        sl = slice(h * LANES, (h + 1) * LANES)
        k_ref[:, sl] = _rms_rows(kv[:, sl], kn_ref[...])
    v_ref[...] = kv[:, 512:1024]


def _memkv(mem, W, depth):
    n, d = mem.shape
    mlen = 256
    batch = n // mlen
    lay = lambda l, b: (l, 0, 0)
    return pl.pallas_call(
        _memkv_kernel,
        grid=(depth, batch),
        in_specs=[pl.BlockSpec((mlen, d), lambda l, b: (b, 0)),
                  pl.BlockSpec((None, 1, d), lay),
                  pl.BlockSpec((None, d, 1024), lay),
                  pl.BlockSpec((None, 1, LANES), lay)],
        out_specs=[pl.BlockSpec((None, mlen, 512), lambda l, b: (l, b, 0))] * 2,
        out_shape=[jax.ShapeDtypeStruct((depth, n, 512), F32)] * 2,
        compiler_params=_params(2, 32),
        name="memkv",
    )(mem, W["norm_mem"], W["w_xkv"], W["kn_x"])


def _front_kernel(*refs, n_in, bodies, prev_a, prev_b):
    ends = np.cumsum(n_in)
    proj_in, a_in, b_in, c_in, d_in = (refs[e - n:e] for e, n in zip(ends, n_in))
    (o_ref, kva_ref, kvb_ref, sc_ref, sd_ref,
     qa_s, hist_a, qb_s, hist_b, zc_s, zd_s, stc_ref, std_ref) = refs[ends[-1]:]
    *c_in, s0c_ref = c_in
    *d_in, s0d_ref = d_in
    tq = qa_s.shape[0]
    i = pl.program_id(1)

    @pl.when(i == 0)
    def _():
        hist_a[0:prev_a, :] = jnp.zeros((prev_a, hist_a.shape[1]), F32)
        hist_b[0:prev_b, :] = jnp.zeros((prev_b, hist_b.shape[1]), F32)
        stc_ref[...] = s0c_ref[...]
        std_ref[...] = s0d_ref[...]

    cur_a = slice(prev_a, prev_a + tq)
    cur_b = slice(prev_b, prev_b + tq)
    dst = dict(a_q=qa_s, a_k=hist_a.at[cur_a, 0:256], a_v=hist_a.at[cur_a, 256:512],
               b_q=qb_s, b_k=hist_b.at[cur_b, 0:512], b_v=hist_b.at[cur_b, 512:1024],
               c=zc_s, d=zd_s)
    body_a, body_b, body_c, body_d = bodies
    (sink_ref,), (bias_ref,) = a_in, b_in
    n_prev_b = prev_b // tq
    k_b = [hist_b.at[j * tq:(j + 1) * tq, 0:512] for j in range(n_prev_b + 1)]
    v_b = [hist_b.at[j * tq:(j + 1) * tq, 512:1024] for j in range(n_prev_b + 1)]
    starts = {
        2: lambda: body_a(qa_s, hist_a.at[0:prev_a, 0:256], hist_a.at[cur_a, 0:256],
                          hist_a.at[0:prev_a, 256:512], hist_a.at[cur_a, 256:512], sink_ref,
                          o_ref.at[:, 0:512]),
        3: lambda: body_b(qb_s, *k_b, *v_b, bias_ref, o_ref.at[:, 512:1024]),
        4: lambda: body_c(zc_s, *c_in, o_ref.at[:, 1024:1536], stc_ref),
        5: lambda: body_d(zd_s, *d_in, o_ref.at[:, 1536:2048], std_ref),
    }
    proj = _inproj_stages(*proj_in, dst)
    running = [proj]
    proj_stage = 0
    while running:
        for gen in list(running):
            if next(gen, "done") == "done":
                running.remove(gen)
            if gen is proj:
                proj_stage += 1
                if proj_stage in starts:
                    running.append(starts[proj_stage]())

    low = _lane_iota((1, LANES)) < HD
    for t, col0 in enumerate((0, 256)):
        kva_ref[:, t * LANES:(t + 1) * LANES] = jnp.where(
            low, hist_a[cur_a, col0:col0 + LANES], hist_a[cur_a, col0 + LANES:col0 + 2 * LANES])
    kvb_ref[...] = hist_b[cur_b, :]
    hist_a[0:prev_a, :] = hist_a[tq:tq + prev_a, :]
    hist_b[0:prev_b, :] = hist_b[tq:tq + prev_b, :]

    @pl.when(i == pl.num_programs(1) - 1)
    def _():
        sc_ref[...] = stc_ref[...]
        sd_ref[...] = std_ref[...]


def _front(x, l, W, cos, sin, relbias, batch, seq):
    n, d = x.shape
    tq = SEQ_TILE
    n_tiles = seq // tq
    cur = lambda b, i: b * n_tiles + i
    sink_rows = jnp.broadcast_to(jnp.repeat(W["sink"][l].reshape(2, 1, 4), CHUNK, axis=2),
                                 (2, SUB, 4 * CHUNK))
    bias_b = _rel_bias(relbias, tq, B_PREV, transposed=True) * LOG2E + jnp.asarray(
        np.where(_band_mask(tq, B_PREV, 8), 0.0, NEG).T[None], F32)
    zero_state = jnp.zeros((batch, C_HEADS, LANES, LANES), F32)
    whole = lambda a: _const_spec(a.shape, (0,) * a.ndim)
    ret_consts = _ret_constants(tq)
    gla_consts = _gla_constants(tq)
    gain = _const_spec((None, 1, 512), (l, 0, 0))
    parts = [
        (_inproj_specs(l, d, tq, n_tiles, cur), _inproj_args(x, W, cos, sin)),
        ([whole(sink_rows)], [sink_rows]),
        ([whole(bias_b)], [bias_b]),
        ([whole(c) for c in ret_consts] + [gain, _state_spec()],
         [*ret_consts, W["gn_c"], zero_state]),
        ([whole(c) for c in gla_consts] + [gain, _state_spec()],
         [*gla_consts, W["gn_d"], zero_state]),
    ]
    bodies = (functools.partial(_attn_a_cols_kernel, masked_prefix=(A_PREV, tq)),
              functools.partial(_attn_b_cols_kernel, n_kv=B_PREV // tq + 1,
                                masked_prefix=(B_PREV, tq)),
              _ret_kernel,
              functools.partial(_gla_kernel, levels=_gla_levels(tq)))
    row_spec = lambda w: pl.BlockSpec((tq, w), lambda b, i: (cur(b, i), 0))
    state_shape = jax.ShapeDtypeStruct((batch, C_HEADS, LANES, LANES), F32)
    return pl.pallas_call(
        functools.partial(_front_kernel, n_in=tuple(len(p[0]) for p in parts), bodies=bodies,
                          prev_a=A_PREV, prev_b=B_PREV),
        grid=(batch, n_tiles),
        in_specs=[s for p in parts for s in p[0]],
        out_specs=[row_spec(2048), row_spec(256), row_spec(1024), _state_spec(), _state_spec()],
        out_shape=[jax.ShapeDtypeStruct((n, 2048), BF16), jax.ShapeDtypeStruct((n, 256), F32),
                   jax.ShapeDtypeStruct((n, 1024), F32), state_shape, state_shape],
        scratch_shapes=[pltpu.VMEM((tq, 512), F32), pltpu.VMEM((A_PREV + tq, 512), F32),
                        pltpu.VMEM((tq, 512), F32), pltpu.VMEM((B_PREV + tq, 1024), F32),
                        pltpu.VMEM((tq, 1536), F32), pltpu.VMEM((tq, 1792), F32),
                        pltpu.VMEM((C_HEADS, LANES, LANES), F32),
                        pltpu.VMEM((C_HEADS, LANES, LANES), F32)],
        compiler_params=_params(2, 60),
        name="front",
    )(*[a for p in parts for a in p[1]])


def _rope_tables(pos):
    half = HD // 2
    freqs = ROPE_THETA ** (-jnp.arange(half, dtype=F32) / half)
    ang = pos.astype(F32)[:, None] * freqs[None, :]
    cos = jnp.tile(jnp.cos(ang), (1, LANES // half))
    sin = jnp.sin(ang)
    sin = jnp.tile(jnp.concatenate([-sin, sin], axis=1), (1, LANES // HD))
    return cos, sin


def _dup_heads(w, col0, n_heads):
    lead = w.shape[:-1]
    blk = w[..., col0:col0 + n_heads * HD].reshape(*lead, n_heads, 1, HD)
    return jnp.broadcast_to(blk, (*lead, n_heads, 2, HD)).reshape(*lead, n_heads * 2 * HD)


def _prepare_weights(P):
    depth, d, _ = P["w_in"].shape
    w_in = P["w_in"]
    vec = lambda a: a.reshape(depth, 1, -1).astype(F32)
    W = {}
    W["wa"] = jnp.concatenate([w_in[..., 0:512], _dup_heads(w_in, 512, 2), _dup_heads(w_in, 640, 2)],
                              axis=-1).astype(BF16)
    W["wb"] = w_in[..., 768:2304].astype(BF16)
    W["wc"] = w_in[..., 2304:3840].astype(BF16)
    W["wd"] = jnp.pad(w_in[..., 3840:5392], ((0, 0), (0, 0), (0, LANES - GLA_RANK))).astype(BF16)
    W["wup"] = jnp.pad(P["w_alpha_up"], ((0, 0), (0, LANES - GLA_RANK), (0, 0))).astype(BF16)
    W["b_alpha"] = vec(P["b_alpha"])
    W["norm_mix"] = vec(P["norm_mix"])
    W["gqa"] = vec(jnp.tile(P["qn_a"], (1, 8)) * (HD ** -0.5 * LOG2E))
    W["gka"] = vec(jnp.tile(P["kn_a"], (1, 4)))
    W["gqb"] = vec(jnp.tile(P["qn_b"], (1, 8)) * (HD ** -0.5 * LOG2E))
    W["gkb"] = vec(jnp.tile(P["kn_b"], (1, 8)))
    W["gn_c"] = vec(P["gn_c"])
    W["gn_d"] = vec(P["gn_d"])
    W["sink"] = P["sink_a"].astype(F32) * LOG2E
    for name in ("w_merge", "w_branch", "w_out", "w_xq", "w_xkv", "w_xo", "w_up", "w_down"):
        W[name] = P[name].astype(BF16)
    for name in ("norm_x", "norm_mem", "qn_x", "kn_x", "norm_ffn"):
        W[name] = vec(P[name])
    return W


def _band_mask(q_rows, prev_rows, n_prev_chunks):
    r = np.arange(q_rows)[:, None] // CHUNK
    u = np.arange(prev_rows + q_rows)[None, :] // CHUNK - prev_rows // CHUNK
    return (u <= r) & (u >= r - n_prev_chunks)


def _rel_bias(table, q_rows, prev_rows, transposed=False):
    nk = prev_rows + q_rows
    period = q_rows + nk
    rows, cols = (nk, q_rows) if transposed else (q_rows, nk)
    j = np.arange(period)
    step = np.where(j < cols, j, j - period)
    rel = prev_rows + step if transposed else prev_rows - step
    v = table[:, np.clip(rel, -REL_CLIP, REL_CLIP) + REL_CLIP]
    skew = jnp.tile(v, (1, rows))[:, :rows * (period - 1)]
    return skew.reshape(table.shape[0], rows, period - 1)[:, :, :cols]


def _mixers_sample(slabs, l, W, relbias, cache, batch, seq):
    za, zb, zc, zd = slabs
    win_k, win_v, band_k, band_v, s_c0, s_d0 = cache
    cur = lambda b, i: b
    la = win_k.shape[1]
    lb = band_k.shape[2]
    ck = jnp.broadcast_to(win_k[:, :, :, None, :], (batch, la, 2, 2, HD)).reshape(batch * la, 256)
    cv = jnp.broadcast_to(win_v[:, :, :, None, :], (batch, la, 2, 2, HD)).reshape(batch * la, 256)
    kv_a = [((ck, cv), la, 0, 0, 2, cur), ((za, za), seq, 512, 768, 2, cur)]
    part_a = _attention_operands(za, kv_a, _stacked_sinks(W["sink"][l], seq), seq, cur)
    body_a = functools.partial(_attn_a_kernel, n_kv=len(kv_a), chunk_rows=seq,
                               keys_per_chunk=la + seq, masked_prefix=None)

    band = (band_k.reshape(-1, 512), band_v.reshape(-1, 512))
    kv_b = [(band, lb, 0, 0, 4, lambda b, i: l * batch + b), ((zb, zb), seq, 512, 1024, 4, cur)]
    part_b = _attention_operands(zb, kv_b, _rel_bias(relbias, seq, lb) * LOG2E, seq, cur)
    body_b = functools.partial(_attn_b_kernel, n_kv=len(kv_b), masked_prefix=None)

    part_c = _recurrent_operands(zc, _ret_constants(seq), l, W["gn_c"], _state_to_kernel(s_c0),
                                 seq, 1)
    part_d = _recurrent_operands(zd, _gla_constants(seq), l, W["gn_d"], _state_to_kernel(s_d0),
                                 seq, 1)
    body_d = functools.partial(_gla_kernel, levels=_gla_levels(seq))
    return _mixers_call((part_a, part_b, part_c, part_d), (body_a, body_b, _ret_kernel, body_d),
                        (batch, 1), seq, cur, 40, "mixers_step")


def _tail_rows(kv_a, kv_b, batch, seq, rows_a, rows_b):
    ta = kv_a.reshape(batch, seq, -1)[:, seq - rows_a:]
    tb = kv_b.reshape(batch, seq, -1)[:, seq - rows_b:]
    ka = ta[:, :, 0:128].reshape(batch, rows_a, 2, HD)
    va = ta[:, :, 128:256].reshape(batch, rows_a, 2, HD)
    kb = tb[:, :, 0:512].reshape(batch, rows_b, 8, HD)
    vb = tb[:, :, 512:1024].reshape(batch, rows_b, 8, HD)
    return ka, va, kb, vb


def _slab_kv(slabs):
    za, zb = slabs[0], slabs[1]
    pick = lambda col0: jnp.concatenate([za[:, col0:col0 + HD], za[:, col0 + LANES:col0 + LANES + HD]], 1)
    return jnp.concatenate([pick(512), pick(768)], axis=1), zb[:, 512:1536]


def kernel(x_prompt, x_sample, cache_win_k, cache_win_v, cache_band_k, cache_band_v, state_ret, state_gla, cache_mem_k, cache_mem_v, mem_prompt, norm_mix, w_in, qn_a, kn_a, sink_a, qn_b, kn_b, relbias_b, gn_c, w_alpha_up, b_alpha, gn_d, w_branch, w_merge, w_out, norm_x, norm_mem, w_xq, w_xkv, qn_x, kn_x, w_xo, norm_ffn, w_up, w_down):
    P = dict(norm_mix=norm_mix, w_in=w_in, qn_a=qn_a, kn_a=kn_a, sink_a=sink_a, qn_b=qn_b,
             kn_b=kn_b, relbias_b=relbias_b, gn_c=gn_c, w_alpha_up=w_alpha_up, b_alpha=b_alpha,
             gn_d=gn_d, w_branch=w_branch, w_merge=w_merge, w_out=w_out, norm_x=norm_x,
             norm_mem=norm_mem, w_xq=w_xq, w_xkv=w_xkv, qn_x=qn_x, kn_x=kn_x, w_xo=w_xo,
             norm_ffn=norm_ffn, w_up=w_up, w_down=w_down)
    depth = w_in.shape[0]
    bp, tp, d = x_prompt.shape
    bs, ts, _ = x_sample.shape
    mlen = mem_prompt.shape[1]
    W = _prepare_weights(P)

    cos_p, sin_p = _rope_tables(jnp.arange(tp))
    cos_s, sin_s = _rope_tables(jnp.tile(PAST_LEN + jnp.arange(ts), bs))
    mem_k, mem_v = _memkv(mem_prompt.reshape(bp * mlen, d), W, depth)
    mem_k2 = mem_k.reshape(depth * bp * mlen, 512)
    mem_v2 = mem_v.reshape(depth * bp * mlen, 512)

    xp = x_prompt.reshape(bp * tp, d)
    xs = x_sample.reshape(bs * ts, d)
    tw = min(TOKEN_TILE, tp)
    n_wide = tp // tw
    rows_p, rows_s = [], []
    st_p, st_s = [], []
    for l in range(depth):
        relbias = relbias_b[l].astype(F32)
        outs, kv_a, kv_b, *states = _front(xp, l, W, cos_p, sin_p, relbias, bp, tp)
        rows_p.append(_tail_rows(kv_a, kv_b, bp, tp, min(A_PREV, tp), min(B_PREV, tp)))
        st_p.append(states)
        xp = _merge(xp, outs, l, W, tw)
        xp = _cross_ffn(xp, mem_k2, mem_v2, l, W, (bp, n_wide), tw,
                        lambda b, i: b * n_wide + i, 1, mlen, lambda b, i: l * bp + b)
        slabs = _inproj(xs, l, W, cos_s, sin_s, 1, bs * ts)
        cache = (cache_win_k[l], cache_win_v[l], cache_band_k, cache_band_v,
                 state_ret[l], state_gla[l])
        outs, *states = _mixers_sample(slabs, l, W, relbias, cache, bs, ts)
        rows_s.append(_tail_rows(*_slab_kv(slabs), bs, ts, ts, ts))
        st_s.append(states)
        xs = _merge(xs, outs, l, W, bs * ts)
        xs = _cross_ffn(xs, cache_mem_k.reshape(-1, 512), cache_mem_v.reshape(-1, 512), l, W,
                        (1, 1), bs * ts, lambda b, i: 0, bs, bs * mlen, lambda b, i: l)

    def stack_rows(rows, k):
        return jnp.stack([r[k] for r in rows])

    def stack_state(states, k):
        return jnp.stack([_state_from_kernel(s[k]) for s in states])

    return (xp.reshape(bp, tp, d), xs.reshape(bs, ts, d),
            stack_rows(rows_p, 0), stack_rows(rows_p, 1), stack_rows(rows_p, 2), stack_rows(rows_p, 3),
            stack_state(st_p, 0), stack_state(st_p, 1),
            mem_k.reshape(depth, bp, mlen, 4, LANES), mem_v.reshape(depth, bp, mlen, 4, LANES),
            stack_rows(rows_s, 0), stack_rows(rows_s, 1), stack_rows(rows_s, 2), stack_rows(rows_s, 3),
            stack_state(st_s, 0), stack_state(st_s, 1))
```

```python
import functools

import numpy as np
import jax
import jax.numpy as jnp
from jax import lax
from jax.experimental import pallas as pl
from jax.experimental.pallas import tpu as pltpu

F32 = jnp.float32
BF16 = jnp.bfloat16

EPS = 1e-6
NEG = -1e30
LOG2E = 1.4426950408889634
PAST_LEN = 1024
CHUNK = 64
ROPE_THETA = 10000.0
A_PREV = 2 * CHUNK
B_PREV = 8 * CHUNK
REL_CLIP = 128
GLA_RANK = 16
GLA_TAU = 16.0
C_HEADS = 4
LANES = 128
SUB = 8
HD = 64
SEQ_TILE = 256
TOKEN_TILE = 512
MIB = 2 ** 20


def _dot(a, b):
    return jnp.dot(a.astype(BF16), b.astype(BF16), preferred_element_type=F32)


def _dot_nt(a, b):
    return lax.dot_general(a.astype(BF16), b.astype(BF16), (((1,), (1,)), ((), ())),
                           preferred_element_type=F32)


def _dot_tn(a, b):
    return lax.dot_general(a.astype(BF16), b.astype(BF16), (((0,), (0,)), ((), ())),
                           preferred_element_type=F32)


def _split3(x):
    x1 = x.astype(BF16)
    r1 = x - x1.astype(F32)
    x2 = r1.astype(BF16)
    x3 = (r1 - x2.astype(F32)).astype(BF16)
    return x1, x2, x3


def _dot_01_left(m01, x):
    x1, x2, x3 = _split3(x)
    dot = functools.partial(jnp.dot, preferred_element_type=F32)
    return dot(m01, x1) + dot(m01, x2) + dot(m01, x3)


def _rms_rows(x, g):
    return x * lax.rsqrt(jnp.mean(x * x, axis=-1, keepdims=True) + EPS) * g


def _lane_iota(shape):
    return lax.broadcasted_iota(jnp.int32, shape, len(shape) - 1)


def _group_ones(width):
    r = lax.broadcasted_iota(jnp.int32, (width, width), 0) // HD
    c = lax.broadcasted_iota(jnp.int32, (width, width), 1) // HD
    return jnp.where(r == c, 1.0, 0.0).astype(BF16)


def _head64_norm(x, g, ones):
    sq = x * x
    hi = sq.astype(BF16)
    lo = (sq - hi.astype(F32)).astype(BF16)
    ss = (jnp.dot(hi, ones, preferred_element_type=F32)
          + jnp.dot(lo, ones, preferred_element_type=F32))
    return x * lax.rsqrt(ss * (1.0 / HD) + EPS) * g


def _rope64(x, cos, sin):
    first_half = (_lane_iota((1, LANES)) % HD) < (HD // 2)
    swapped = jnp.where(first_half, pltpu.roll(x, LANES - HD // 2, 1), pltpu.roll(x, HD // 2, 1))
    return x * cos + swapped * sin


def _sigmoid(x):
    return 1.0 / (1.0 + jnp.exp(-x))


def _log_sigmoid(x):
    return jnp.minimum(x, 0.0) - jnp.log(1.0 + jnp.exp(-jnp.abs(x)))


def _params(n_axes, vmem_mib):
    return pltpu.CompilerParams(dimension_semantics=("arbitrary",) * n_axes,
                                vmem_limit_bytes=vmem_mib * MIB)


def _const_spec(shape, index):
    return pl.BlockSpec(shape, lambda *_: index, pipeline_mode=pl.Buffered(1))


def _inproj_stages(x_ref, g_ref, wa_ref, wb0_ref, wb1_ref, wc0_ref, wc1_ref, wd0_ref, wd1_ref,
                   wad_ref, wup_ref, ba_ref, gqa_ref, gka_ref, gqb_ref, gkb_ref, cos_ref, sin_ref,
                   dst):
    h = _rms_rows(x_ref[...], g_ref[...]).astype(BF16)

    def project(w0_ref, w1_ref):
        return jnp.concatenate([jnp.dot(h, w0_ref[...], preferred_element_type=F32),
                                jnp.dot(h, w1_ref[...], preferred_element_type=F32)], axis=1)

    cos = cos_ref[...]
    sin = sin_ref[...]
    pair = 2 * LANES
    ones = _group_ones(pair)

    za = jnp.dot(h, wa_ref[...], preferred_element_type=F32)
    zb = project(wb0_ref, wb1_ref)
    yield
    for p in range(3):
        sl = slice(p * pair, (p + 1) * pair)
        gain = gqa_ref[:, sl] if p < 2 else gka_ref[...]
        y = _head64_norm(za[:, sl], gain, ones)
        ref, col0 = (dst["a_q"], p * pair) if p < 2 else (dst["a_k"], 0)
        for c in range(2):
            ref[:, col0 + c * LANES:col0 + (c + 1) * LANES] = _rope64(
                y[:, c * LANES:(c + 1) * LANES], cos, sin)
    dst["a_v"][...] = za[:, 768:1024]
    zc = project(wc0_ref, wc1_ref)
    yield
    for p in range(4):
        sl = slice(p * pair, (p + 1) * pair)
        if p < 2:
            dst["b_q"][:, sl] = _head64_norm(zb[:, sl], gqb_ref[:, sl], ones)
        else:
            ksl = slice((p - 2) * pair, (p - 1) * pair)
            dst["b_k"][:, ksl] = _head64_norm(zb[:, sl], gkb_ref[:, ksl], ones)
    dst["b_v"][...] = zb[:, 1024:1536]
    zd = project(wd0_ref, wd1_ref)
    ad = jnp.dot(h, wad_ref[...], preferred_element_type=F32)
    yield
    oc_ref = dst["c"]
    for c in range(2):
        sl = slice(c * LANES, (c + 1) * LANES)
        oc_ref[:, sl] = _rope64(zc[:, sl], cos, sin)
    for c in range(2):
        sl = slice(256 + c * LANES, 256 + (c + 1) * LANES)
        oc_ref[:, sl] = _rope64(zc[:, sl], cos, sin) * (HD ** -0.5)
    oc_ref[:, 512:1536] = zc[:, 512:1536]
    yield
    od_ref = dst["d"]
    od_ref[:, 0:256] = zd[:, 0:256] * (HD ** -0.5)
    od_ref[:, 256:1536] = zd[:, 256:1536]
    pre = _dot(ad, wup_ref[...]) + ba_ref[...]
    od_ref[:, 1536:1792] = _log_sigmoid(pre) * (LOG2E / GLA_TAU)


def _inproj_kernel(*refs):
    *ins, oa_ref, ob_ref, oc_ref, od_ref = refs
    dst = dict(a_q=oa_ref.at[:, 0:512], a_k=oa_ref.at[:, 512:768], a_v=oa_ref.at[:, 768:1024],
               b_q=ob_ref.at[:, 0:512], b_k=ob_ref.at[:, 512:1024], b_v=ob_ref.at[:, 1024:1536],
               c=oc_ref, d=od_ref)
    for _ in _inproj_stages(*ins, dst):
        pass


def _inproj_specs(l, d, tm, n_pos_blocks, row_index):
    lay = (l, 0, 0)
    return [
        pl.BlockSpec((tm, d), lambda *g: (row_index(*g), 0)),
        _const_spec((None, 1, d), lay),
        _const_spec((None, d, 1024), lay),
        *[_const_spec((None, d, 768), (l, 0, j)) for j in range(1, 7)],
        _const_spec((None, d, LANES), lay),
        _const_spec((None, LANES, 256), lay),
        _const_spec((None, 1, 256), lay),
        _const_spec((None, 1, 512), lay),
        _const_spec((None, 1, 256), lay),
        _const_spec((None, 1, 512), lay),
        _const_spec((None, 1, 512), lay),
        pl.BlockSpec((tm, LANES), lambda *g: (row_index(*g) % n_pos_blocks, 0)),
        pl.BlockSpec((tm, LANES), lambda *g: (row_index(*g) % n_pos_blocks, 0)),
    ]


def _inproj_args(x, W, cos, sin):
    return [x, W["norm_mix"], W["wa"], *[W["w_in"]] * 6, W["wad"], W["wup"], W["b_alpha"],
            W["gqa"], W["gka"], W["gqb"], W["gkb"], cos, sin]


def _inproj(x, l, W, cos, sin, n_pos_blocks, tm):
    n, d = x.shape
    row = lambda i: (i, 0)
    widths = (1024, 1536, 1536, 1792)
    return pl.pallas_call(
        _inproj_kernel,
        grid=(n // tm,),
        in_specs=_inproj_specs(l, d, tm, n_pos_blocks, lambda i: i),
        out_specs=[pl.BlockSpec((tm, w), row) for w in widths],
        out_shape=[jax.ShapeDtypeStruct((n, w), F32) for w in widths],
        compiler_params=_params(1, 56),
        name="inproj",
    )(*_inproj_args(x, W, cos, sin))


def _stack_rows(refs):
    blocks = [r[...].astype(BF16) for r in refs]
    return blocks[0] if len(blocks) == 1 else jnp.concatenate(blocks, axis=0)


def _first_valid_key(masked_prefix):
    prev_rows, seq_tile = masked_prefix
    return prev_rows - pl.program_id(1) * seq_tile


def _attn_a_kernel(*refs, n_kv, chunk_rows, keys_per_chunk, masked_prefix):
    q_ref = refs[0]
    k_all = _stack_rows(refs[1:1 + n_kv])
    v_all = _stack_rows(refs[1 + n_kv:1 + 2 * n_kv])
    sink_ref = refs[1 + 2 * n_kv]
    o_ref = refs[2 + 2 * n_kv]
    cr = chunk_rows
    low = _lane_iota((1, LANES)) < HD
    units = [(j, g) for j in range(q_ref.shape[0] // cr) for g in range(2)]
    keys_of = lambda j: slice(j * cr, j * cr + keys_per_chunk)
    group = lambda g: slice(g * LANES, (g + 1) * LANES)
    scores = []
    for j, g in units:
        stacked = []
        for c in (2 * g, 2 * g + 1):
            q_c = q_ref[j * cr:(j + 1) * cr, c * LANES:(c + 1) * LANES]
            stacked += [jnp.where(low, q_c, 0.0), jnp.where(low, 0.0, q_c)]
        s = _dot_nt(jnp.concatenate(stacked, axis=0), k_all[keys_of(j), group(g)])
        if masked_prefix is not None:
            key_ok = _lane_iota((1, keys_per_chunk)) >= _first_valid_key(masked_prefix) - j * cr
            s = jnp.where(key_ok, s, NEG)
        scores.append(s)
    yield
    probs = []
    for (j, g), s in zip(units, scores):
        sink = sink_ref[g][:, 0:1]
        m = jnp.maximum(jnp.max(s, axis=-1, keepdims=True), sink)
        p = jnp.exp2(s - m)
        probs.append((p.astype(BF16), jnp.sum(p, axis=-1, keepdims=True) + jnp.exp2(sink - m)))
    yield
    outs = [_dot(p, v_all[keys_of(j), group(g)]) / den for (j, g), (p, den) in zip(units, probs)]
    yield
    for (j, g), o in zip(units, outs):
        for t, c in enumerate((2 * g, 2 * g + 1)):
            o_ref[j * cr:(j + 1) * cr, c * LANES:(c + 1) * LANES] = jnp.where(
                low, o[2 * t * cr:(2 * t + 1) * cr],
                o[(2 * t + 1) * cr:(2 * t + 2) * cr]).astype(o_ref.dtype)


def _attn_a_cols_kernel(q_ref, kp_ref, kc_ref, vp_ref, vc_ref, sink_ref, o_ref, *, masked_prefix):
    prev_rows = kp_ref.shape[0]
    k_all = _stack_rows((kp_ref, kc_ref))
    v_all = _stack_rows((vp_ref, vc_ref))
    cr = CHUNK
    keys_per_chunk = prev_rows + cr
    lane_low = _lane_iota((1, LANES)) < HD
    same_half = ((lax.broadcasted_iota(jnp.int32, (LANES, LANES), 0) < HD)
                 == (_lane_iota((LANES, LANES)) < HD))
    key_row = lax.broadcasted_iota(jnp.int32, (keys_per_chunk, 2 * LANES), 0)
    units = [(j, g) for j in range(q_ref.shape[0] // cr) for g in range(2)]
    keys_of = lambda j: slice(j * cr, j * cr + keys_per_chunk)
    group = lambda g: slice(g * LANES, (g + 1) * LANES)

    scores = []
    for j, g in units:
        q_t = []
        for c in (2 * g, 2 * g + 1):
            q_c = q_ref[j * cr:(j + 1) * cr, c * LANES:(c + 1) * LANES]
            both = jnp.concatenate([q_c, q_c], axis=0).T
            q_t.append(jnp.where(same_half, both, 0.0).astype(BF16))
        s_t = jnp.dot(k_all[keys_of(j), group(g)], jnp.concatenate(q_t, axis=1),
                      preferred_element_type=F32)
        if j * cr < prev_rows:
            s_t = jnp.where(key_row >= _first_valid_key(masked_prefix) - j * cr, s_t, NEG)
        scores.append(s_t)
    yield
    probs = []
    for (j, g), s_t in zip(units, scores):
        sink = sink_ref[g][0:1, :]
        m = jnp.maximum(jnp.max(s_t, axis=0, keepdims=True), sink)
        p_t = jnp.exp2(s_t - m)
        den = jnp.sum(p_t, axis=0, keepdims=True) + jnp.exp2(sink - m)
        probs.append((p_t.astype(BF16), den))
    yield
    outs = []
    for (j, g), (p_t, den) in zip(units, probs):
        outs.append(_dot_tn(v_all[keys_of(j), group(g)], p_t) / den)
    yield
    for (j, g), o_t in zip(units, outs):
        for t, c in enumerate((2 * g, 2 * g + 1)):
            blk = o_t[:, t * LANES:(t + 1) * LANES].T
            o_ref[j * cr:(j + 1) * cr, c * LANES:(c + 1) * LANES] = jnp.where(
                lane_low, blk[0:cr], blk[cr:2 * cr]).astype(o_ref.dtype)


def _attn_b_kernel(*refs, n_kv, masked_prefix):
    q_ref = refs[0]
    k_all = _stack_rows(refs[1:1 + n_kv])
    v_all = _stack_rows(refs[1 + n_kv:1 + 2 * n_kv])
    bias_ref = refs[1 + 2 * n_kv]
    o_ref = refs[2 + 2 * n_kv]
    low = _lane_iota((1, LANES)) < HD
    if masked_prefix is not None:
        key_ok = _lane_iota((1, k_all.shape[0])) >= _first_valid_key(masked_prefix)
    slab = lambda c: slice(c * LANES, (c + 1) * LANES)
    heads = [(c, half) for c in range(4) for half in range(2)]
    scores = []
    for c, half in heads:
        q_c = q_ref[:, slab(c)]
        q_h = jnp.where(low, q_c, 0.0) if half == 0 else jnp.where(low, 0.0, q_c)
        scores.append(_dot_nt(q_h, k_all[:, slab(c)]))
    yield
    probs = []
    for (c, half), s in zip(heads, scores):
        s = s + bias_ref[2 * c + half]
        if masked_prefix is not None:
            s = jnp.where(key_ok, s, NEG)
        p = jnp.exp2(s - jnp.max(s, axis=-1, keepdims=True))
        probs.append((p.astype(BF16), jnp.sum(p, axis=-1, keepdims=True)))
    yield
    outs = [_dot(p, v_all[:, slab(c)]) / den for (c, half), (p, den) in zip(heads, probs)]
    yield
    for c in range(4):
        o_ref[:, slab(c)] = jnp.where(low, outs[2 * c], outs[2 * c + 1]).astype(o_ref.dtype)


def _attention_operands(q_arr, kv_arrs, extra, q_rows, q_index):
    in_specs = [pl.BlockSpec((q_rows, 512), lambda b, i: (q_index(b, i), 0))]
    args = [q_arr]
    for use_v in (False, True):
        for arrs, rows, kc0, vc0, ncol, index_fn in kv_arrs:
            width = ncol * LANES
            col_block = (vc0 if use_v else kc0) // width
            in_specs.append(pl.BlockSpec(
                (rows, width),
                functools.partial(lambda b, i, f, cb: (f(b, i), cb), f=index_fn, cb=col_block)))
            args.append(arrs[1] if use_v else arrs[0])
    in_specs.append(_const_spec(extra.shape, (0,) * extra.ndim))
    args.append(extra)
    return in_specs, args


def _stacked_sinks(sink, chunk_rows):
    col = jnp.repeat(sink.reshape(2, 4), chunk_rows, axis=1)
    return jnp.broadcast_to(col[:, :, None], (2, 4 * chunk_rows, LANES)).astype(F32)


def _head_lane_mask(head):
    low = _lane_iota((1, LANES)) < HD
    return low if head % 2 == 0 else jnp.logical_not(low)


def _ret_kernel(z_ref, dmat_ref, qdec_ref, kdec_ref, sdec_ref, gn_ref, o_ref, st_ref):
    q = z_ref[:, 0:256]
    k = z_ref[:, 256:512]
    q_in = q * qdec_ref[...]
    k_st = k * kdec_ref[...]
    heads = range(C_HEADS)
    pair = lambda h: slice((h // 2) * LANES, (h // 2 + 1) * LANES)
    v = [z_ref[:, 512 + h * LANES:512 + (h + 1) * LANES].astype(BF16) for h in heads]
    old_state = [st_ref[h] for h in heads]
    raw = [_dot_nt(q[:, pair(h)], jnp.where(_head_lane_mask(h), k[:, pair(h)], 0.0)) for h in heads]
    inter = [_dot_nt(q_in[:, pair(h)], old_state[h]) for h in heads]
    state_upd = [_dot_tn(v[h], jnp.where(_head_lane_mask(h), k_st[:, pair(h)], 0.0)) for h in heads]
    yield
    scores = [(raw[h] * dmat_ref[h]).astype(BF16) for h in heads]
    yield
    outs = [jnp.dot(scores[h], v[h], preferred_element_type=F32) + inter[h] for h in heads]
    yield
    for h in heads:
        o = outs[h]
        g_h = z_ref[:, 1024 + h * LANES:1024 + (h + 1) * LANES]
        oc = o - jnp.mean(o, axis=-1, keepdims=True)
        y = oc * lax.rsqrt(jnp.mean(oc * oc, axis=-1, keepdims=True) + EPS)
        y = y * gn_ref[:, h * LANES:(h + 1) * LANES]
        o_ref[:, h * LANES:(h + 1) * LANES] = (y * (g_h * _sigmoid(g_h))).astype(o_ref.dtype)
        st_ref[h] = old_state[h] * sdec_ref[:, pair(h)] + state_upd[h]


def _gla_kernel(z_ref, tril_ref, lmask_ref, bd_ref, ind_ref, rep_ref, gn_ref, o_ref, st_ref, *,
                levels):
    tc = z_ref.shape[0]
    diag = bd_ref.shape[0]
    q = z_ref[:, 0:256]
    k = z_ref[:, 256:512]
    la = z_ref[:, 1536:1792]
    cum = _dot_01_left(tril_ref[...], la)
    cum_last = cum[tc - 1:tc, :]
    yield

    off_diag = []
    for size in (s for s in levels if s > diag):
        for start in range(0, tc, size):
            m = start + size // 2 - 1
            rows = slice(start + size // 2, start + size)
            cols = slice(start, start + size // 2)
            q_r = q[rows] * jnp.exp2(cum[rows] - cum[m:m + 1])
            k_c = k[cols] * jnp.exp2(cum[m:m + 1] - cum[cols])
            off_diag.append((rows, cols, q_r, k_c))
    small = [s for s in levels if s <= diag]
    q_lv, k_lv = [], []
    for size in small:
        mids = []
        for start in range(0, tc, size):
            m = start + size // 2 - 1
            mids.append(jnp.broadcast_to(cum[m:m + 1, :], (size, 256)))
        cm = mids[0] if len(mids) == 1 else jnp.concatenate(mids, axis=0)
        e = jnp.exp2(-jnp.abs(cum - cm))
        q_lv.append(q * e)
        k_lv.append(k * e)

    nb = tc // SUB
    q3 = q.reshape(nb, SUB, 256)
    k3 = k.reshape(nb, SUB, 256)
    c3 = cum.reshape(nb, SUB, 256)
    row_in_block = lax.broadcasted_iota(jnp.int32, (1, SUB, 1), 1)
    r = None
    for j in range(SUB):
        e = jnp.exp2(jnp.where(row_in_block >= j, c3 - c3[:, j:j + 1, :], NEG))
        t = q3 * k3[:, j:j + 1, :] * e
        part = jnp.dot(t.reshape(tc, 256).astype(BF16), ind_ref[j], preferred_element_type=F32)
        r = part if r is None else r + part
    r = r.astype(BF16)
    bd = bd_ref[...]
    yield

    q_in = q * jnp.exp2(cum)
    k_st = k * jnp.exp2(cum_last - cum)
    s_dec = jnp.exp2(cum_last)

    heads = range(C_HEADS)
    pair = lambda h: slice((h // 2) * LANES, (h // 2 + 1) * LANES)
    block = lambda b: slice(b * diag, (b + 1) * diag)
    n_blocks = tc // diag
    v = [z_ref[:, 512 + h * LANES:512 + (h + 1) * LANES].astype(BF16) for h in heads]
    old_state = [st_ref[h] for h in heads]
    inter = [_dot_nt(q_in[:, pair(h)], old_state[h]) for h in heads]
    state_upd = [_dot_tn(v[h], jnp.where(_head_lane_mask(h), k_st[:, pair(h)], 0.0)) for h in heads]
    raw = {}
    for h in heads:
        hm = _head_lane_mask(h)
        for b in range(n_blocks):
            raw[h, b, "blk8"] = jnp.dot(r[block(b)], rep_ref[h], preferred_element_type=F32)
            for lv in range(len(small)):
                k_h = jnp.where(hm, k_lv[lv][block(b), pair(h)], 0.0)
                raw[h, b, lv] = _dot_nt(q_lv[lv][block(b), pair(h)], k_h)
        for n, (rows, cols, q_r, k_c) in enumerate(off_diag):
            raw[h, "off", n] = _dot_nt(q_r[:, pair(h)], jnp.where(hm, k_c[:, pair(h)], 0.0))
    yield
    scores = {}
    for h in heads:
        for b in range(n_blocks):
            s_b = raw[h, b, "blk8"] * bd
            for lv in range(len(small)):
                s_b = s_b + raw[h, b, lv] * lmask_ref[lv]
            scores[h, b] = s_b.astype(BF16)
    yield
    outs = []
    for h in heads:
        blocks = [jnp.dot(scores[h, b], v[h][block(b)], preferred_element_type=F32)
                  for b in range(n_blocks)]
        for n, (rows, cols, _, _) in enumerate(off_diag):
            o_rc = _dot(raw[h, "off", n], v[h][cols])
            for b in range(rows.start // diag, rows.stop // diag):
                lo = b * diag - rows.start
                blocks[b] = blocks[b] + o_rc[lo:lo + diag]
        o = blocks[0] if n_blocks == 1 else jnp.concatenate(blocks, axis=0)
        outs.append(o + inter[h])
    yield
    for h in heads:
        o = outs[h]
        g_h = z_ref[:, 1024 + h * LANES:1024 + (h + 1) * LANES]
        y = o * lax.rsqrt(jnp.mean(o * o, axis=-1, keepdims=True) + EPS)
        y = y * gn_ref[:, h * LANES:(h + 1) * LANES]
        o_ref[:, h * LANES:(h + 1) * LANES] = (y * (g_h * _sigmoid(g_h))).astype(o_ref.dtype)
        st_ref[h] = old_state[h] * s_dec[:, pair(h)] + state_upd[h]


def _ret_constants(tc):
    log_gamma = np.log1p(-np.exp2(-5.0 - np.arange(C_HEADS, dtype=np.float64)))
    idx = np.arange(tc)
    diff = idx[:, None] - idx[None, :]
    dmat = np.where(diff >= 0, np.exp(log_gamma[:, None, None] * np.maximum(diff, 0)[None]), 0.0)
    per_lane = np.repeat(log_gamma, HD)
    qdec = np.exp((idx[:, None] + 1) * per_lane[None, :])
    kdec = np.exp((tc - 1 - idx[:, None]) * per_lane[None, :])
    sdec = np.exp(tc * per_lane)[None, :]
    return [jnp.asarray(a, F32) for a in (dmat, qdec, kdec, sdec)]


def _gla_levels(tc):
    levels = []
    size = tc
    while size >= 2 * SUB:
        levels.append(size)
        size //= 2
    return tuple(levels)


def _gla_diag(tc):
    return min(tc, LANES)


def _gla_constants(tc):
    diag = _gla_diag(tc)
    idx = np.arange(tc)
    tril = (idx[None, :] <= idx[:, None]).astype(np.float32)
    idx = np.arange(diag)
    i, j = idx[:, None], idx[None, :]
    lmasks = []
    for size in (s for s in _gla_levels(tc) if s <= diag):
        lmasks.append(((i // size == j // size) & (i % size >= size // 2) & (j % size < size // 2)))
    lmask = np.stack(lmasks).astype(np.float32)
    bd = (i // SUB == j // SUB).astype(np.float32)
    ind = np.zeros((SUB, 256, LANES), np.float32)
    for jj in range(SUB):
        for h in range(C_HEADS):
            ind[jj, h * HD:(h + 1) * HD, h * SUB + jj] = 1.0
    rep = np.zeros((C_HEADS, LANES, diag), np.float32)
    for h in range(C_HEADS):
        for jj in range(SUB):
            rep[h, h * SUB + jj, jj::SUB] = 1.0
    return (jnp.asarray(tril, BF16), jnp.asarray(lmask, F32), jnp.asarray(bd, F32),
            jnp.asarray(ind, BF16), jnp.asarray(rep, BF16))


def _state_spec():
    return pl.BlockSpec((None, C_HEADS, LANES, LANES), lambda b, i: (b, 0, 0, 0))


def _recurrent_operands(z, consts, l, gn, s0, tc, n_tiles):
    in_specs = [pl.BlockSpec((tc, z.shape[1]), lambda b, i: (b * n_tiles + i, 0))]
    in_specs += [_const_spec(c.shape, (0,) * c.ndim) for c in consts]
    in_specs += [_const_spec((None, 1, 512), (l, 0, 0)), _state_spec()]
    return in_specs, [z, *consts, gn, s0]


def _mixers_kernel(*refs, bodies, n_in):
    ends = np.cumsum(n_in)
    a_in, b_in, c_in, d_in = (refs[e - n:e] for e, n in zip(ends, n_in))
    o_ref, sc_ref, sd_ref, stc_ref, std_ref = refs[ends[-1]:]
    *c_in, s0c_ref = c_in
    *d_in, s0d_ref = d_in
    i = pl.program_id(1)

    @pl.when(i == 0)
    def _():
        stc_ref[...] = s0c_ref[...]
        std_ref[...] = s0d_ref[...]

    body_a, body_b, body_c, body_d = bodies
    running = [body_d(*d_in, o_ref.at[:, 1536:2048], std_ref),
               body_b(*b_in, o_ref.at[:, 512:1024]),
               body_a(*a_in, o_ref.at[:, 0:512]),
               body_c(*c_in, o_ref.at[:, 1024:1536], stc_ref)]
    while running:
        for gen in list(running):
            if next(gen, "done") == "done":
                running.remove(gen)

    @pl.when(i == pl.num_programs(1) - 1)
    def _():
        sc_ref[...] = stc_ref[...]
        sd_ref[...] = std_ref[...]


def _mixers_call(parts, bodies, grid, rows, row_index, vmem_mib, name):
    in_specs = [s for specs, _ in parts for s in specs]
    args = [a for _, arrs in parts for a in arrs]
    n_in = tuple(len(specs) for specs, _ in parts)
    n_rows = grid[0] * grid[1] * rows
    state_shape = jax.ShapeDtypeStruct((grid[0], C_HEADS, LANES, LANES), F32)
    return pl.pallas_call(
        functools.partial(_mixers_kernel, bodies=bodies, n_in=n_in),
        grid=grid,
        in_specs=in_specs,
        out_specs=[pl.BlockSpec((rows, 2048), lambda b, i: (row_index(b, i), 0)),
                   _state_spec(), _state_spec()],
        out_shape=[jax.ShapeDtypeStruct((n_rows, 2048), BF16), state_shape, state_shape],
        scratch_shapes=[pltpu.VMEM((C_HEADS, LANES, LANES), F32)] * 2,
        compiler_params=_params(2, vmem_mib),
        name=name,
    )(*args)


def _state_to_kernel(s):
    st = jnp.swapaxes(s, -1, -2)
    lo = jnp.pad(st, ((0, 0), (0, 0), (0, 0), (0, HD)))
    hi = jnp.pad(st, ((0, 0), (0, 0), (0, 0), (HD, 0)))
    odd = (jnp.arange(C_HEADS) % 2 == 1)[None, :, None, None]
    return jnp.where(odd, hi, lo)


def _state_from_kernel(st):
    lo = st[..., :HD]
    hi = st[..., HD:]
    odd = (jnp.arange(C_HEADS) % 2 == 1)[None, :, None, None]
    return jnp.swapaxes(jnp.where(odd, hi, lo), -1, -2)


def _merge_kernel(x_ref, g_ref, o_ref, wm_ref, wbr_ref, wo_ref, y_ref):
    x = x_ref[...]
    d = x.shape[1]
    h = _rms_rows(x, g_ref[...]).astype(BF16)
    acc = None
    for b in range(4):
        gate = _sigmoid(jnp.dot(h, wm_ref[:, b * d:(b + 1) * d], preferred_element_type=F32))
        term = gate * jnp.dot(o_ref[:, b * 512:(b + 1) * 512], wbr_ref[b],
                              preferred_element_type=F32)
        acc = term if acc is None else acc + term
    y_ref[...] = x + _dot(acc, wo_ref[...])


def _merge(x, outs, l, W, tm):
    n, d = x.shape
    row = lambda i: (i, 0)
    lay = (l, 0, 0)
    in_specs = [pl.BlockSpec((tm, d), row), _const_spec((None, 1, d), lay),
                pl.BlockSpec((tm, 2048), row),
                _const_spec((None, d, 4 * d), lay),
                _const_spec((None, 4, 512, d), (l, 0, 0, 0)),
                _const_spec((None, d, d), lay)]
    return pl.pallas_call(
        _merge_kernel,
        grid=(n // tm,),
        in_specs=in_specs,
        out_specs=pl.BlockSpec((tm, d), row),
        out_shape=jax.ShapeDtypeStruct((n, d), F32),
        compiler_params=_params(1, 48),
        name="merge",
    )(x, W["norm_mix"], outs, W["w_merge"], W["w_branch"], W["w_out"])


def _cross_ffn_kernel(x_ref, gx_ref, wq_ref, qn_ref, mk_ref, mv_ref, wxo_ref, gf_ref, wup_ref,
                      wdn_ref, y_ref, *, n_seq):
    x = x_ref[...]
    d = x.shape[1]
    rows = x.shape[0] // n_seq
    mlen = mk_ref.shape[0] // n_seq
    hx = _rms_rows(x, gx_ref[...])
    q = _dot(hx, wq_ref[...])
    slab = lambda h: slice(h * LANES, (h + 1) * LANES)
    units = [(h, b) for h in range(4) for b in range(n_seq)]
    mem_of = lambda b: slice(b * mlen, (b + 1) * mlen)
    q_n = [(_rms_rows(q[:, slab(h)], qn_ref[...]) * (LANES ** -0.5 * LOG2E)).astype(BF16)
           for h in range(4)]
    scores = [_dot_nt(q_n[h][b * rows:(b + 1) * rows], mk_ref[mem_of(b), slab(h)])
              for h, b in units]
    probs = []
    for s in scores:
        p = jnp.exp2(s - jnp.max(s, axis=-1, keepdims=True))
        probs.append((p.astype(BF16), jnp.sum(p, axis=-1, keepdims=True)))
    outs = [_dot(p, mv_ref[mem_of(b), slab(h)]) / den for (h, b), (p, den) in zip(units, probs)]
    heads = []
    for h in range(4):
        per_seq = outs[h * n_seq:(h + 1) * n_seq]
        heads.append(per_seq[0] if n_seq == 1 else jnp.concatenate(per_seq, axis=0))
    x = x + _dot(jnp.concatenate(heads, axis=1), wxo_ref[...])
    hf = _rms_rows(x, gf_ref[...]).astype(BF16)
    acc = None
    for j in range(wup_ref.shape[1] // d):
        sl = slice(j * d, (j + 1) * d)
        u = jnp.maximum(jnp.dot(hf, wup_ref[:, sl], preferred_element_type=F32), 0.0)
        term = _dot(u * u, wdn_ref[sl, :])
        acc = term if acc is None else acc + term
    y_ref[...] = x + acc


def _cross_ffn(x, mk, mv, l, W, grid, rows, x_index, n_seq, mem_rows, mem_index):
    n, d = x.shape
    lay = (l, 0, 0)
    dff = W["w_up"].shape[-1]
    in_specs = [pl.BlockSpec((rows, d), lambda b, i: (x_index(b, i), 0)),
                _const_spec((None, 1, d), lay),
                _const_spec((None, d, 512), lay),
                _const_spec((None, 1, LANES), lay),
                pl.BlockSpec((mem_rows, 512), lambda b, i: (mem_index(b, i), 0)),
                pl.BlockSpec((mem_rows, 512), lambda b, i: (mem_index(b, i), 0)),
                _const_spec((None, 512, d), lay),
                _const_spec((None, 1, d), lay),
                _const_spec((None, d, dff), lay),
                _const_spec((None, dff, d), lay)]
    return pl.pallas_call(
        functools.partial(_cross_ffn_kernel, n_seq=n_seq),
        grid=grid,
        in_specs=in_specs,
        out_specs=pl.BlockSpec((rows, d), lambda b, i: (x_index(b, i), 0)),
        out_shape=jax.ShapeDtypeStruct((n, d), F32),
        compiler_params=_params(2, 48),
        name="cross_ffn",
    )(x, W["norm_x"], W["w_xq"], W["qn_x"], mk, mv, W["w_xo"], W["norm_ffn"], W["w_up"],
      W["w_down"])


def _memkv_kernel(mem_ref, g_ref, w_ref, kn_ref, k_ref, v_ref):
    hm = _rms_rows(mem_ref[...], g_ref[...])
    kv = _dot(hm, w_ref[...])
    for h in range(4):
        sl = slice(h * LANES, (h + 1) * LANES)
        k_ref[:, sl] = _rms_rows(kv[:, sl], kn_ref[...])
    v_ref[...] = kv[:, 512:1024]


def _memkv(mem, W, depth):
    n, d = mem.shape
    mlen = 256
    batch = n // mlen
    lay = lambda l, b: (l, 0, 0)
    return pl.pallas_call(
        _memkv_kernel,
        grid=(depth, batch),
        in_specs=[pl.BlockSpec((mlen, d), lambda l, b: (b, 0)),
                  pl.BlockSpec((None, 1, d), lay),
                  pl.BlockSpec((None, d, 1024), lay),
                  pl.BlockSpec((None, 1, LANES), lay)],
        out_specs=[pl.BlockSpec((None, mlen, 512), lambda l, b: (l, b, 0))] * 2,
        out_shape=[jax.ShapeDtypeStruct((depth, n, 512), F32)] * 2,
        compiler_params=_params(2, 32),
        name="memkv",
    )(mem, W["norm_mem"], W["w_xkv"], W["kn_x"])


def _front_kernel(*refs, n_in, bodies, prev_a, prev_b):
    ends = np.cumsum(n_in)
    proj_in, a_in, b_in, c_in, d_in = (refs[e - n:e] for e, n in zip(ends, n_in))
    (o_ref, kva_ref, kvb_ref, sc_ref, sd_ref,
     qa_s, hist_a, qb_s, hist_b, zc_s, zd_s, stc_ref, std_ref) = refs[ends[-1]:]
    *c_in, s0c_ref = c_in
    *d_in, s0d_ref = d_in
    tq = qa_s.shape[0]
    i = pl.program_id(1)

    @pl.when(i == 0)
    def _():
        hist_a[0:prev_a, :] = jnp.zeros((prev_a, hist_a.shape[1]), F32)
        hist_b[0:prev_b, :] = jnp.zeros((prev_b, hist_b.shape[1]), F32)
        stc_ref[...] = s0c_ref[...]
        std_ref[...] = s0d_ref[...]

    cur_a = slice(prev_a, prev_a + tq)
    cur_b = slice(prev_b, prev_b + tq)
    dst = dict(a_q=qa_s, a_k=hist_a.at[cur_a, 0:256], a_v=hist_a.at[cur_a, 256:512],
               b_q=qb_s, b_k=hist_b.at[cur_b, 0:512], b_v=hist_b.at[cur_b, 512:1024],
               c=zc_s, d=zd_s)
    body_a, body_b, body_c, body_d = bodies
    (sink_ref,), (bias_ref,) = a_in, b_in
    n_prev_b = prev_b // tq
    k_b = [hist_b.at[j * tq:(j + 1) * tq, 0:512] for j in range(n_prev_b + 1)]
    v_b = [hist_b.at[j * tq:(j + 1) * tq, 512:1024] for j in range(n_prev_b + 1)]
    starts = {
        2: lambda: body_a(qa_s, hist_a.at[0:prev_a, 0:256], hist_a.at[cur_a, 0:256],
                          hist_a.at[0:prev_a, 256:512], hist_a.at[cur_a, 256:512], sink_ref,
                          o_ref.at[:, 0:512]),
        3: lambda: body_b(qb_s, *k_b, *v_b, bias_ref, o_ref.at[:, 512:1024]),
        4: lambda: body_c(zc_s, *c_in, o_ref.at[:, 1024:1536], stc_ref),
        5: lambda: body_d(zd_s, *d_in, o_ref.at[:, 1536:2048], std_ref),
    }
    proj = _inproj_stages(*proj_in, dst)
    running = [proj]
    proj_stage = 0
    while running:
        for gen in list(running):
            if next(gen, "done") == "done":
                running.remove(gen)
            if gen is proj:
                proj_stage += 1
                if proj_stage in starts:
                    running.append(starts[proj_stage]())

    low = _lane_iota((1, LANES)) < HD
    for t, col0 in enumerate((0, 256)):
        kva_ref[:, t * LANES:(t + 1) * LANES] = jnp.where(
            low, hist_a[cur_a, col0:col0 + LANES], hist_a[cur_a, col0 + LANES:col0 + 2 * LANES])
    kvb_ref[...] = hist_b[cur_b, :]
    hist_a[0:prev_a, :] = hist_a[tq:tq + prev_a, :]
    hist_b[0:prev_b, :] = hist_b[tq:tq + prev_b, :]

    @pl.when(i == pl.num_programs(1) - 1)
    def _():
        sc_ref[...] = stc_ref[...]
        sd_ref[...] = std_ref[...]


def _front(x, l, W, cos, sin, relbias, batch, seq):
    n, d = x.shape
    tq = SEQ_TILE
    n_tiles = seq // tq
    cur = lambda b, i: b * n_tiles + i
    sink_rows = jnp.broadcast_to(jnp.repeat(W["sink"][l].reshape(2, 1, 4), CHUNK, axis=2),
                                 (2, SUB, 4 * CHUNK))
    bias_b = _rel_bias(relbias, tq, B_PREV) * LOG2E + jnp.asarray(
        np.where(_band_mask(tq, B_PREV, 8), 0.0, NEG)[None], F32)
    zero_state = jnp.zeros((batch, C_HEADS, LANES, LANES), F32)
    whole = lambda a: _const_spec(a.shape, (0,) * a.ndim)
    ret_consts = _ret_constants(tq)
    gla_consts = _gla_constants(tq)
    gain = _const_spec((None, 1, 512), (l, 0, 0))
    parts = [
        (_inproj_specs(l, d, tq, n_tiles, cur), _inproj_args(x, W, cos, sin)),
        ([whole(sink_rows)], [sink_rows]),
        ([whole(bias_b)], [bias_b]),
        ([whole(c) for c in ret_consts] + [gain, _state_spec()],
         [*ret_consts, W["gn_c"], zero_state]),
        ([whole(c) for c in gla_consts] + [gain, _state_spec()],
         [*gla_consts, W["gn_d"], zero_state]),
    ]
    bodies = (functools.partial(_attn_a_cols_kernel, masked_prefix=(A_PREV, tq)),
              functools.partial(_attn_b_kernel, n_kv=B_PREV // tq + 1, masked_prefix=(B_PREV, tq)),
              _ret_kernel,
              functools.partial(_gla_kernel, levels=_gla_levels(tq)))
    row_spec = lambda w: pl.BlockSpec((tq, w), lambda b, i: (cur(b, i), 0))
    state_shape = jax.ShapeDtypeStruct((batch, C_HEADS, LANES, LANES), F32)
    return pl.pallas_call(
        functools.partial(_front_kernel, n_in=tuple(len(p[0]) for p in parts), bodies=bodies,
                          prev_a=A_PREV, prev_b=B_PREV),
        grid=(batch, n_tiles),
        in_specs=[s for p in parts for s in p[0]],
        out_specs=[row_spec(2048), row_spec(256), row_spec(1024), _state_spec(), _state_spec()],
        out_shape=[jax.ShapeDtypeStruct((n, 2048), BF16), jax.ShapeDtypeStruct((n, 256), F32),
                   jax.ShapeDtypeStruct((n, 1024), F32), state_shape, state_shape],
        scratch_shapes=[pltpu.VMEM((tq, 512), F32), pltpu.VMEM((A_PREV + tq, 512), F32),
                        pltpu.VMEM((tq, 512), F32), pltpu.VMEM((B_PREV + tq, 1024), F32),
                        pltpu.VMEM((tq, 1536), F32), pltpu.VMEM((tq, 1792), F32),
                        pltpu.VMEM((C_HEADS, LANES, LANES), F32),
                        pltpu.VMEM((C_HEADS, LANES, LANES), F32)],
        compiler_params=_params(2, 60),
        name="front",
    )(*[a for p in parts for a in p[1]])


def _rope_tables(pos):
    half = HD // 2
    freqs = ROPE_THETA ** (-jnp.arange(half, dtype=F32) / half)
    ang = pos.astype(F32)[:, None] * freqs[None, :]
    cos = jnp.tile(jnp.cos(ang), (1, LANES // half))
    sin = jnp.sin(ang)
    sin = jnp.tile(jnp.concatenate([-sin, sin], axis=1), (1, LANES // HD))
    return cos, sin


def _dup_heads(w, col0, n_heads):
    lead = w.shape[:-1]
    blk = w[..., col0:col0 + n_heads * HD].reshape(*lead, n_heads, 1, HD)
    return jnp.broadcast_to(blk, (*lead, n_heads, 2, HD)).reshape(*lead, n_heads * 2 * HD)


def _prepare_weights(P):
    depth, d, _ = P["w_in"].shape
    w_in = P["w_in"]
    vec = lambda a: a.reshape(depth, 1, -1).astype(F32)
    W = {}
    W["wa"] = jnp.concatenate([w_in[..., 0:512], _dup_heads(w_in, 512, 2), _dup_heads(w_in, 640, 2)],
                              axis=-1).astype(BF16)
    W["w_in"] = w_in.astype(BF16)
    W["wad"] = jnp.pad(w_in[..., 5376:5392], ((0, 0), (0, 0), (0, LANES - GLA_RANK))).astype(BF16)
    W["wup"] = jnp.pad(P["w_alpha_up"], ((0, 0), (0, LANES - GLA_RANK), (0, 0))).astype(BF16)
    W["b_alpha"] = vec(P["b_alpha"])
    W["norm_mix"] = vec(P["norm_mix"])
    W["gqa"] = vec(jnp.tile(P["qn_a"], (1, 8)) * (HD ** -0.5 * LOG2E))
    W["gka"] = vec(jnp.tile(P["kn_a"], (1, 4)))
    W["gqb"] = vec(jnp.tile(P["qn_b"], (1, 8)) * (HD ** -0.5 * LOG2E))
    W["gkb"] = vec(jnp.tile(P["kn_b"], (1, 8)))
    W["gn_c"] = vec(P["gn_c"])
    W["gn_d"] = vec(P["gn_d"])
    W["sink"] = P["sink_a"].astype(F32) * LOG2E
    for name in ("w_merge", "w_branch", "w_out", "w_xq", "w_xkv", "w_xo", "w_up", "w_down"):
        W[name] = P[name].astype(BF16)
    for name in ("norm_x", "norm_mem", "qn_x", "kn_x", "norm_ffn"):
        W[name] = vec(P[name])
    return W


def _band_mask(q_rows, prev_rows, n_prev_chunks):
    r = np.arange(q_rows)[:, None] // CHUNK
    u = np.arange(prev_rows + q_rows)[None, :] // CHUNK - prev_rows // CHUNK
    return (u <= r) & (u >= r - n_prev_chunks)


def _rel_bias(table, q_rows, prev_rows):
    nk = prev_rows + q_rows
    period = q_rows + nk
    j = np.arange(period)
    offset = np.where(j < nk, j, j - period)
    idx = np.clip(prev_rows - offset, -REL_CLIP, REL_CLIP) + REL_CLIP
    v = table[:, idx]
    skew = jnp.tile(v, (1, q_rows))[:, :q_rows * (period - 1)]
    return skew.reshape(table.shape[0], q_rows, period - 1)[:, :, :nk]


def _mixers_sample(slabs, l, W, relbias, cache, batch, seq):
    za, zb, zc, zd = slabs
    win_k, win_v, band_k, band_v, s_c0, s_d0 = cache
    cur = lambda b, i: b
    la = win_k.shape[1]
    lb = band_k.shape[2]
    ck = jnp.broadcast_to(win_k[:, :, :, None, :], (batch, la, 2, 2, HD)).reshape(batch * la, 256)
    cv = jnp.broadcast_to(win_v[:, :, :, None, :], (batch, la, 2, 2, HD)).reshape(batch * la, 256)
    kv_a = [((ck, cv), la, 0, 0, 2, cur), ((za, za), seq, 512, 768, 2, cur)]
    part_a = _attention_operands(za, kv_a, _stacked_sinks(W["sink"][l], seq), seq, cur)
    body_a = functools.partial(_attn_a_kernel, n_kv=len(kv_a), chunk_rows=seq,
                               keys_per_chunk=la + seq, masked_prefix=None)

    band = (band_k.reshape(-1, 512), band_v.reshape(-1, 512))
    kv_b = [(band, lb, 0, 0, 4, lambda b, i: l * batch + b), ((zb, zb), seq, 512, 1024, 4, cur)]
    part_b = _attention_operands(zb, kv_b, _rel_bias(relbias, seq, lb) * LOG2E, seq, cur)
    body_b = functools.partial(_attn_b_kernel, n_kv=len(kv_b), masked_prefix=None)

    part_c = _recurrent_operands(zc, _ret_constants(seq), l, W["gn_c"], _state_to_kernel(s_c0),
                                 seq, 1)
    part_d = _recurrent_operands(zd, _gla_constants(seq), l, W["gn_d"], _state_to_kernel(s_d0),
                                 seq, 1)
    body_d = functools.partial(_gla_kernel, levels=_gla_levels(seq))
    return _mixers_call((part_a, part_b, part_c, part_d), (body_a, body_b, _ret_kernel, body_d),
                        (batch, 1), seq, cur, 40, "mixers_step")


def _tail_rows(kv_a, kv_b, batch, seq, rows_a, rows_b):
    ta = kv_a.reshape(batch, seq, -1)[:, seq - rows_a:]
    tb = kv_b.reshape(batch, seq, -1)[:, seq - rows_b:]
    ka = ta[:, :, 0:128].reshape(batch, rows_a, 2, HD)
    va = ta[:, :, 128:256].reshape(batch, rows_a, 2, HD)
    kb = tb[:, :, 0:512].reshape(batch, rows_b, 8, HD)
    vb = tb[:, :, 512:1024].reshape(batch, rows_b, 8, HD)
    return ka, va, kb, vb


def _slab_kv(slabs):
    za, zb = slabs[0], slabs[1]
    pick = lambda col0: jnp.concatenate([za[:, col0:col0 + HD], za[:, col0 + LANES:col0 + LANES + HD]], 1)
    return jnp.concatenate([pick(512), pick(768)], axis=1), zb[:, 512:1536]


def kernel(x_prompt, x_sample, cache_win_k, cache_win_v, cache_band_k, cache_band_v, state_ret, state_gla, cache_mem_k, cache_mem_v, mem_prompt, norm_mix, w_in, qn_a, kn_a, sink_a, qn_b, kn_b, relbias_b, gn_c, w_alpha_up, b_alpha, gn_d, w_branch, w_merge, w_out, norm_x, norm_mem, w_xq, w_xkv, qn_x, kn_x, w_xo, norm_ffn, w_up, w_down):
    P = dict(norm_mix=norm_mix, w_in=w_in, qn_a=qn_a, kn_a=kn_a, sink_a=sink_a, qn_b=qn_b,
             kn_b=kn_b, relbias_b=relbias_b, gn_c=gn_c, w_alpha_up=w_alpha_up, b_alpha=b_alpha,
             gn_d=gn_d, w_branch=w_branch, w_merge=w_merge, w_out=w_out, norm_x=norm_x,
             norm_mem=norm_mem, w_xq=w_xq, w_xkv=w_xkv, qn_x=qn_x, kn_x=kn_x, w_xo=w_xo,
             norm_ffn=norm_ffn, w_up=w_up, w_down=w_down)
    depth = w_in.shape[0]
    bp, tp, d = x_prompt.shape
    bs, ts, _ = x_sample.shape
    mlen = mem_prompt.shape[1]
    W = _prepare_weights(P)

    cos_p, sin_p = _rope_tables(jnp.arange(tp))
    cos_s, sin_s = _rope_tables(jnp.tile(PAST_LEN + jnp.arange(ts), bs))
    mem_k, mem_v = _memkv(mem_prompt.reshape(bp * mlen, d), W, depth)
    mem_k2 = mem_k.reshape(depth * bp * mlen, 512)
    mem_v2 = mem_v.reshape(depth * bp * mlen, 512)

    xp = x_prompt.reshape(bp * tp, d)
    xs = x_sample.reshape(bs * ts, d)
    tw = min(TOKEN_TILE, tp)
    n_wide = tp // tw
    rows_p, rows_s = [], []
    st_p, st_s = [], []
    for l in range(depth):
        relbias = relbias_b[l].astype(F32)
        outs, kv_a, kv_b, *states = _front(xp, l, W, cos_p, sin_p, relbias, bp, tp)
        rows_p.append(_tail_rows(kv_a, kv_b, bp, tp, min(A_PREV, tp), min(B_PREV, tp)))
        st_p.append(states)
        xp = _merge(xp, outs, l, W, tw)
        xp = _cross_ffn(xp, mem_k2, mem_v2, l, W, (bp, n_wide), tw,
                        lambda b, i: b * n_wide + i, 1, mlen, lambda b, i: l * bp + b)
        slabs = _inproj(xs, l, W, cos_s, sin_s, 1, bs * ts)
        cache = (cache_win_k[l], cache_win_v[l], cache_band_k, cache_band_v,
                 state_ret[l], state_gla[l])
        outs, *states = _mixers_sample(slabs, l, W, relbias, cache, bs, ts)
        rows_s.append(_tail_rows(*_slab_kv(slabs), bs, ts, ts, ts))
        st_s.append(states)
        xs = _merge(xs, outs, l, W, bs * ts)
        xs = _cross_ffn(xs, cache_mem_k.reshape(-1, 512), cache_mem_v.reshape(-1, 512), l, W,
                        (1, 1), bs * ts, lambda b, i: 0, bs, bs * mlen, lambda b, i: l)

    def stack_rows(rows, k):
        return jnp.stack([r[k] for r in rows])

    def stack_state(states, k):
        return jnp.stack([_state_from_kernel(s[k]) for s in states])

    return (xp.reshape(bp, tp, d), xs.reshape(bs, ts, d),
            stack_rows(rows_p, 0), stack_rows(rows_p, 1), stack_rows(rows_p, 2), stack_rows(rows_p, 3),
            stack_state(st_p, 0), stack_state(st_p, 1),
            mem_k.reshape(depth, bp, mlen, 4, LANES), mem_v.reshape(depth, bp, mlen, 4, LANES),
            stack_rows(rows_s, 0), stack_rows(rows_s, 1), stack_rows(rows_s, 2), stack_rows(rows_s, 3),
            stack_state(st_s, 0), stack_state(st_s, 1))
```

```python
import functools

import numpy as np
import jax
import jax.numpy as jnp
from jax import lax
from jax.experimental import pallas as pl
from jax.experimental.pallas import tpu as pltpu

F32 = jnp.float32
BF16 = jnp.bfloat16

EPS = 1e-6
NEG = -1e30
LOG2E = 1.4426950408889634
PAST_LEN = 1024
CHUNK = 64
ROPE_THETA = 10000.0
A_PREV = 2 * CHUNK
B_PREV = 8 * CHUNK
REL_CLIP = 128
GLA_RANK = 16
GLA_TAU = 16.0
C_HEADS = 4
LANES = 128
SUB = 8
HD = 64
SEQ_TILE = 256
TOKEN_TILE = 512
MIB = 2 ** 20
VMEM_MIB = 64


def _dot(a, b):
    return jnp.dot(a.astype(BF16), b.astype(BF16), preferred_element_type=F32)


def _dot_nt(a, b):
    return lax.dot_general(a.astype(BF16), b.astype(BF16), (((1,), (1,)), ((), ())),
                           preferred_element_type=F32)


def _dot_tn(a, b):
    return lax.dot_general(a.astype(BF16), b.astype(BF16), (((0,), (0,)), ((), ())),
                           preferred_element_type=F32)


def _split3(x):
    x1 = x.astype(BF16)
    r1 = x - x1.astype(F32)
    x2 = r1.astype(BF16)
    x3 = (r1 - x2.astype(F32)).astype(BF16)
    return x1, x2, x3


def _dot_01_left(m01, x):
    x1, x2, x3 = _split3(x)
    dot = functools.partial(jnp.dot, preferred_element_type=F32)
    return dot(m01, x1) + dot(m01, x2) + dot(m01, x3)


def _rms_rows(x, g):
    return x * lax.rsqrt(jnp.mean(x * x, axis=-1, keepdims=True) + EPS) * g


def _lane_iota(shape):
    return lax.broadcasted_iota(jnp.int32, shape, len(shape) - 1)


def _group_ones(width):
    r = lax.broadcasted_iota(jnp.int32, (width, width), 0) // HD
    c = lax.broadcasted_iota(jnp.int32, (width, width), 1) // HD
    return jnp.where(r == c, 1.0, 0.0).astype(BF16)


def _head64_norm(x, g, ones):
    sq = x * x
    hi = sq.astype(BF16)
    lo = (sq - hi.astype(F32)).astype(BF16)
    ss = (jnp.dot(hi, ones, preferred_element_type=F32)
          + jnp.dot(lo, ones, preferred_element_type=F32))
    return x * lax.rsqrt(ss * (1.0 / HD) + EPS) * g


def _rope64(x, cos, sin):
    first_half = (_lane_iota((1, LANES)) % HD) < (HD // 2)
    swapped = jnp.where(first_half, pltpu.roll(x, LANES - HD // 2, 1), pltpu.roll(x, HD // 2, 1))
    return x * cos + swapped * sin


def _sigmoid(x):
    return 1.0 / (1.0 + jnp.exp(-x))


def _log_sigmoid(x):
    return jnp.minimum(x, 0.0) - jnp.log(1.0 + jnp.exp(-jnp.abs(x)))


def _params(n_axes, vmem_mib):
    assert vmem_mib < VMEM_MIB
    return pltpu.CompilerParams(dimension_semantics=("arbitrary",) * n_axes,
                                vmem_limit_bytes=vmem_mib * MIB)


def _const_spec(shape, index):
    return pl.BlockSpec(shape, lambda *_: index, pipeline_mode=pl.Buffered(1))


def _inproj_stages(x_ref, g_ref, wa_ref, wb0_ref, wb1_ref, wc0_ref, wc1_ref, wd0_ref, wd1_ref,
                   wad_ref, wup_ref, ba_ref, gqa_ref, gka_ref, gqb_ref, gkb_ref, cos_ref, sin_ref,
                   dst):
    h = _rms_rows(x_ref[...], g_ref[...]).astype(BF16)

    def project(w0_ref, w1_ref):
        return jnp.concatenate([jnp.dot(h, w0_ref[...], preferred_element_type=F32),
                                jnp.dot(h, w1_ref[...], preferred_element_type=F32)], axis=1)

    cos = cos_ref[...]
    sin = sin_ref[...]
    pair = 2 * LANES
    ones = _group_ones(pair)

    za = jnp.dot(h, wa_ref[...], preferred_element_type=F32)
    zb = project(wb0_ref, wb1_ref)
    yield
    for p in range(3):
        sl = slice(p * pair, (p + 1) * pair)
        gain = gqa_ref[:, sl] if p < 2 else gka_ref[...]
        y = _head64_norm(za[:, sl], gain, ones)
        ref, col0 = (dst["a_q"], p * pair) if p < 2 else (dst["a_k"], 0)
        for c in range(2):
            ref[:, col0 + c * LANES:col0 + (c + 1) * LANES] = _rope64(
                y[:, c * LANES:(c + 1) * LANES], cos, sin)
    dst["a_v"][...] = za[:, 768:1024]
    zc = project(wc0_ref, wc1_ref)
    yield
    for p in range(4):
        sl = slice(p * pair, (p + 1) * pair)
        if p < 2:
            dst["b_q"][:, sl] = _head64_norm(zb[:, sl], gqb_ref[:, sl], ones)
        else:
            ksl = slice((p - 2) * pair, (p - 1) * pair)
            dst["b_k"][:, ksl] = _head64_norm(zb[:, sl], gkb_ref[:, ksl], ones)
    dst["b_v"][...] = zb[:, 1024:1536]
    zd = project(wd0_ref, wd1_ref)
    ad = jnp.dot(h, wad_ref[...], preferred_element_type=F32)
    yield
    oc_ref = dst["c"]
    for c in range(2):
        sl = slice(c * LANES, (c + 1) * LANES)
        oc_ref[:, sl] = _rope64(zc[:, sl], cos, sin)
    for c in range(2):
        sl = slice(256 + c * LANES, 256 + (c + 1) * LANES)
        oc_ref[:, sl] = _rope64(zc[:, sl], cos, sin) * (HD ** -0.5)
    oc_ref[:, 512:1536] = zc[:, 512:1536]
    yield
    od_ref = dst["d"]
    od_ref[:, 0:256] = zd[:, 0:256] * (HD ** -0.5)
    od_ref[:, 256:1536] = zd[:, 256:1536]
    pre = _dot(ad, wup_ref[...]) + ba_ref[...]
    od_ref[:, 1536:1792] = _log_sigmoid(pre) * (LOG2E / GLA_TAU)


def _inproj_kernel(*refs):
    *ins, oa_ref, ob_ref, oc_ref, od_ref = refs
    dst = dict(a_q=oa_ref.at[:, 0:512], a_k=oa_ref.at[:, 512:768], a_v=oa_ref.at[:, 768:1024],
               b_q=ob_ref.at[:, 0:512], b_k=ob_ref.at[:, 512:1024], b_v=ob_ref.at[:, 1024:1536],
               c=oc_ref, d=od_ref)
    for _ in _inproj_stages(*ins, dst):
        pass


def _inproj_specs(l, d, tm, n_pos_blocks, row_index):
    lay = (l, 0, 0)
    return [
        pl.BlockSpec((tm, d), lambda *g: (row_index(*g), 0)),
        _const_spec((None, 1, d), lay),
        _const_spec((None, d, 1024), lay),
        *[_const_spec((None, d, 768), (l, 0, j)) for j in range(1, 7)],
        _const_spec((None, d, LANES), lay),
        _const_spec((None, LANES, 256), lay),
        _const_spec((None, 1, 256), lay),
        _const_spec((None, 1, 512), lay),
        _const_spec((None, 1, 256), lay),
        _const_spec((None, 1, 512), lay),
        _const_spec((None, 1, 512), lay),
        pl.BlockSpec((tm, LANES), lambda *g: (row_index(*g) % n_pos_blocks, 0)),
        pl.BlockSpec((tm, LANES), lambda *g: (row_index(*g) % n_pos_blocks, 0)),
    ]


def _inproj_args(x, W, cos, sin):
    return [x, W["norm_mix"], W["wa"], *[W["w_in"]] * 6, W["wad"], W["wup"], W["b_alpha"],
            W["gqa"], W["gka"], W["gqb"], W["gkb"], cos, sin]


def _inproj(x, l, W, cos, sin, n_pos_blocks, tm):
    n, d = x.shape
    row = lambda i: (i, 0)
    widths = (1024, 1536, 1536, 1792)
    return pl.pallas_call(
        _inproj_kernel,
        grid=(n // tm,),
        in_specs=_inproj_specs(l, d, tm, n_pos_blocks, lambda i: i),
        out_specs=[pl.BlockSpec((tm, w), row) for w in widths],
        out_shape=[jax.ShapeDtypeStruct((n, w), F32) for w in widths],
        compiler_params=_params(1, 56),
        name="inproj",
    )(*_inproj_args(x, W, cos, sin))


def _stack_rows(refs):
    blocks = [r[...].astype(BF16) for r in refs]
    return blocks[0] if len(blocks) == 1 else jnp.concatenate(blocks, axis=0)


def _first_valid_key(masked_prefix):
    prev_rows, seq_tile = masked_prefix
    return prev_rows - pl.program_id(1) * seq_tile


def _attn_a_kernel(*refs, n_kv, chunk_rows, keys_per_chunk, masked_prefix):
    q_ref = refs[0]
    k_all = _stack_rows(refs[1:1 + n_kv])
    v_all = _stack_rows(refs[1 + n_kv:1 + 2 * n_kv])
    sink_ref = refs[1 + 2 * n_kv]
    o_ref = refs[2 + 2 * n_kv]
    cr = chunk_rows
    low = _lane_iota((1, LANES)) < HD
    units = [(j, g) for j in range(q_ref.shape[0] // cr) for g in range(2)]
    keys_of = lambda j: slice(j * cr, j * cr + keys_per_chunk)
    group = lambda g: slice(g * LANES, (g + 1) * LANES)
    scores = []
    for j, g in units:
        stacked = []
        for c in (2 * g, 2 * g + 1):
            q_c = q_ref[j * cr:(j + 1) * cr, c * LANES:(c + 1) * LANES]
            stacked += [jnp.where(low, q_c, 0.0), jnp.where(low, 0.0, q_c)]
        s = _dot_nt(jnp.concatenate(stacked, axis=0), k_all[keys_of(j), group(g)])
        if masked_prefix is not None:
            key_ok = _lane_iota((1, keys_per_chunk)) >= _first_valid_key(masked_prefix) - j * cr
            s = jnp.where(key_ok, s, NEG)
        scores.append(s)
    yield
    probs = []
    for (j, g), s in zip(units, scores):
        sink = sink_ref[g][:, 0:1]
        m = jnp.maximum(jnp.max(s, axis=-1, keepdims=True), sink)
        p = jnp.exp2(s - m)
        probs.append((p.astype(BF16), jnp.sum(p, axis=-1, keepdims=True) + jnp.exp2(sink - m)))
    yield
    outs = [_dot(p, v_all[keys_of(j), group(g)]) / den for (j, g), (p, den) in zip(units, probs)]
    yield
    for (j, g), o in zip(units, outs):
        for t, c in enumerate((2 * g, 2 * g + 1)):
            o_ref[j * cr:(j + 1) * cr, c * LANES:(c + 1) * LANES] = jnp.where(
                low, o[2 * t * cr:(2 * t + 1) * cr],
                o[(2 * t + 1) * cr:(2 * t + 2) * cr]).astype(o_ref.dtype)


def _attn_a_cols_kernel(q_ref, kp_ref, kc_ref, vp_ref, vc_ref, sink_ref, o_ref, *, masked_prefix):
    prev_rows = kp_ref.shape[0]
    k_all = _stack_rows((kp_ref, kc_ref))
    v_all = _stack_rows((vp_ref, vc_ref))
    cr = CHUNK
    keys_per_chunk = prev_rows + cr
    lane_low = _lane_iota((1, LANES)) < HD
    same_half = ((lax.broadcasted_iota(jnp.int32, (LANES, LANES), 0) < HD)
                 == (_lane_iota((LANES, LANES)) < HD))
    key_row = lax.broadcasted_iota(jnp.int32, (keys_per_chunk, 2 * LANES), 0)
    units = [(j, g) for j in range(q_ref.shape[0] // cr) for g in range(2)]
    keys_of = lambda j: slice(j * cr, j * cr + keys_per_chunk)
    group = lambda g: slice(g * LANES, (g + 1) * LANES)

    scores = []
    for j, g in units:
        q_t = []
        for c in (2 * g, 2 * g + 1):
            q_c = q_ref[j * cr:(j + 1) * cr, c * LANES:(c + 1) * LANES]
            both = jnp.concatenate([q_c, q_c], axis=0).T
            q_t.append(jnp.where(same_half, both, 0.0).astype(BF16))
        s_t = jnp.dot(k_all[keys_of(j), group(g)], jnp.concatenate(q_t, axis=1),
                      preferred_element_type=F32)
        if j * cr < prev_rows:
            s_t = jnp.where(key_row >= _first_valid_key(masked_prefix) - j * cr, s_t, NEG)
        scores.append(s_t)
    yield
    probs = []
    for (j, g), s_t in zip(units, scores):
        sink = sink_ref[g][0:1, :]
        m = jnp.maximum(jnp.max(s_t, axis=0, keepdims=True), sink)
        p_t = jnp.exp2(s_t - m)
        den = jnp.sum(p_t, axis=0, keepdims=True) + jnp.exp2(sink - m)
        probs.append((p_t.astype(BF16), den))
    yield
    outs = []
    for (j, g), (p_t, den) in zip(units, probs):
        outs.append(_dot_tn(v_all[keys_of(j), group(g)], p_t) / den)
    yield
    for (j, g), o_t in zip(units, outs):
        for t, c in enumerate((2 * g, 2 * g + 1)):
            blk = o_t[:, t * LANES:(t + 1) * LANES].T
            o_ref[j * cr:(j + 1) * cr, c * LANES:(c + 1) * LANES] = jnp.where(
                lane_low, blk[0:cr], blk[cr:2 * cr]).astype(o_ref.dtype)


def _attn_b_kernel(*refs, n_kv, masked_prefix):
    q_ref = refs[0]
    k_all = _stack_rows(refs[1:1 + n_kv])
    v_all = _stack_rows(refs[1 + n_kv:1 + 2 * n_kv])
    bias_ref = refs[1 + 2 * n_kv]
    o_ref = refs[2 + 2 * n_kv]
    low = _lane_iota((1, LANES)) < HD
    if masked_prefix is not None:
        key_ok = _lane_iota((1, k_all.shape[0])) >= _first_valid_key(masked_prefix)
    slab = lambda c: slice(c * LANES, (c + 1) * LANES)
    heads = [(c, half) for c in range(4) for half in range(2)]
    scores = []
    for c, half in heads:
        q_c = q_ref[:, slab(c)]
        q_h = jnp.where(low, q_c, 0.0) if half == 0 else jnp.where(low, 0.0, q_c)
        scores.append(_dot_nt(q_h, k_all[:, slab(c)]))
    yield
    probs = []
    for (c, half), s in zip(heads, scores):
        s = s + bias_ref[2 * c + half]
        if masked_prefix is not None:
            s = jnp.where(key_ok, s, NEG)
        p = jnp.exp2(s - jnp.max(s, axis=-1, keepdims=True))
        probs.append((p.astype(BF16), jnp.sum(p, axis=-1, keepdims=True)))
    yield
    outs = [_dot(p, v_all[:, slab(c)]) / den for (c, half), (p, den) in zip(heads, probs)]
    yield
    for c in range(4):
        o_ref[:, slab(c)] = jnp.where(low, outs[2 * c], outs[2 * c + 1]).astype(o_ref.dtype)


def _attention_operands(q_arr, kv_arrs, extra, q_rows, q_index):
    in_specs = [pl.BlockSpec((q_rows, 512), lambda b, i: (q_index(b, i), 0))]
    args = [q_arr]
    for use_v in (False, True):
        for arrs, rows, kc0, vc0, ncol, index_fn in kv_arrs:
            width = ncol * LANES
            col_block = (vc0 if use_v else kc0) // width
            in_specs.append(pl.BlockSpec(
                (rows, width),
                functools.partial(lambda b, i, f, cb: (f(b, i), cb), f=index_fn, cb=col_block)))
            args.append(arrs[1] if use_v else arrs[0])
    in_specs.append(_const_spec(extra.shape, (0,) * extra.ndim))
    args.append(extra)
    return in_specs, args


def _stacked_sinks(sink, chunk_rows):
    col = jnp.repeat(sink.reshape(2, 4), chunk_rows, axis=1)
    return jnp.broadcast_to(col[:, :, None], (2, 4 * chunk_rows, LANES)).astype(F32)


def _head_lane_mask(head):
    low = _lane_iota((1, LANES)) < HD
    return low if head % 2 == 0 else jnp.logical_not(low)


def _ret_kernel(z_ref, dmat_ref, qdec_ref, kdec_ref, sdec_ref, gn_ref, o_ref, st_ref):
    q = z_ref[:, 0:256]
    k = z_ref[:, 256:512]
    q_in = q * qdec_ref[...]
    k_st = k * kdec_ref[...]
    heads = range(C_HEADS)
    pair = lambda h: slice((h // 2) * LANES, (h // 2 + 1) * LANES)
    v = [z_ref[:, 512 + h * LANES:512 + (h + 1) * LANES].astype(BF16) for h in heads]
    old_state = [st_ref[h] for h in heads]
    raw = [_dot_nt(q[:, pair(h)], jnp.where(_head_lane_mask(h), k[:, pair(h)], 0.0)) for h in heads]
    inter = [_dot_nt(q_in[:, pair(h)], old_state[h]) for h in heads]
    state_upd = [_dot_tn(v[h], jnp.where(_head_lane_mask(h), k_st[:, pair(h)], 0.0)) for h in heads]
    yield
    scores = [(raw[h] * dmat_ref[h]).astype(BF16) for h in heads]
    yield
    outs = [jnp.dot(scores[h], v[h], preferred_element_type=F32) + inter[h] for h in heads]
    yield
    for h in heads:
        o = outs[h]
        g_h = z_ref[:, 1024 + h * LANES:1024 + (h + 1) * LANES]
        oc = o - jnp.mean(o, axis=-1, keepdims=True)
        y = oc * lax.rsqrt(jnp.mean(oc * oc, axis=-1, keepdims=True) + EPS)
        y = y * gn_ref[:, h * LANES:(h + 1) * LANES]
        o_ref[:, h * LANES:(h + 1) * LANES] = (y * (g_h * _sigmoid(g_h))).astype(o_ref.dtype)
        st_ref[h] = old_state[h] * sdec_ref[:, pair(h)] + state_upd[h]


def _gla_kernel(z_ref, tril_ref, lmask_ref, bd_ref, ind_ref, rep_ref, gn_ref, o_ref, st_ref, *,
                levels):
    tc = z_ref.shape[0]
    diag = bd_ref.shape[0]
    q = z_ref[:, 0:256]
    k = z_ref[:, 256:512]
    la = z_ref[:, 1536:1792]
    cum = _dot_01_left(tril_ref[...], la)
    cum_last = cum[tc - 1:tc, :]
    yield

    off_diag = []
    for size in (s for s in levels if s > diag):
        for start in range(0, tc, size):
            m = start + size // 2 - 1
            rows = slice(start + size // 2, start + size)
            cols = slice(start, start + size // 2)
            q_r = q[rows] * jnp.exp2(cum[rows] - cum[m:m + 1])
            k_c = k[cols] * jnp.exp2(cum[m:m + 1] - cum[cols])
            off_diag.append((rows, cols, q_r, k_c))
    small = [s for s in levels if s <= diag]
    q_lv, k_lv = [], []
    for size in small:
        mids = []
        for start in range(0, tc, size):
            m = start + size // 2 - 1
            mids.append(jnp.broadcast_to(cum[m:m + 1, :], (size, 256)))
        cm = mids[0] if len(mids) == 1 else jnp.concatenate(mids, axis=0)
        e = jnp.exp2(-jnp.abs(cum - cm))
        q_lv.append(q * e)
        k_lv.append(k * e)

    nb = tc // SUB
    q3 = q.reshape(nb, SUB, 256)
    k3 = k.reshape(nb, SUB, 256)
    c3 = cum.reshape(nb, SUB, 256)
    row_in_block = lax.broadcasted_iota(jnp.int32, (1, SUB, 1), 1)
    r = None
    for j in range(SUB):
        e = jnp.exp2(jnp.where(row_in_block >= j, c3 - c3[:, j:j + 1, :], NEG))
        t = q3 * k3[:, j:j + 1, :] * e
        part = jnp.dot(t.reshape(tc, 256).astype(BF16), ind_ref[j], preferred_element_type=F32)
        r = part if r is None else r + part
    r = r.astype(BF16)
    bd = bd_ref[...]
    yield

    q_in = q * jnp.exp2(cum)
    k_st = k * jnp.exp2(cum_last - cum)
    s_dec = jnp.exp2(cum_last)

    heads = range(C_HEADS)
    pair = lambda h: slice((h // 2) * LANES, (h // 2 + 1) * LANES)
    block = lambda b: slice(b * diag, (b + 1) * diag)
    n_blocks = tc // diag
    v = [z_ref[:, 512 + h * LANES:512 + (h + 1) * LANES].astype(BF16) for h in heads]
    old_state = [st_ref[h] for h in heads]
    inter = [_dot_nt(q_in[:, pair(h)], old_state[h]) for h in heads]
    state_upd = [_dot_tn(v[h], jnp.where(_head_lane_mask(h), k_st[:, pair(h)], 0.0)) for h in heads]
    raw = {}
    for h in heads:
        hm = _head_lane_mask(h)
        for b in range(n_blocks):
            raw[h, b, "blk8"] = jnp.dot(r[block(b)], rep_ref[h], preferred_element_type=F32)
            for lv in range(len(small)):
                k_h = jnp.where(hm, k_lv[lv][block(b), pair(h)], 0.0)
                raw[h, b, lv] = _dot_nt(q_lv[lv][block(b), pair(h)], k_h)
        for n, (rows, cols, q_r, k_c) in enumerate(off_diag):
            raw[h, "off", n] = _dot_nt(q_r[:, pair(h)], jnp.where(hm, k_c[:, pair(h)], 0.0))
    yield
    scores = {}
    for h in heads:
        for b in range(n_blocks):
            s_b = raw[h, b, "blk8"] * bd
            for lv in range(len(small)):
                s_b = s_b + raw[h, b, lv] * lmask_ref[lv]
            scores[h, b] = s_b.astype(BF16)
    yield
    outs = []
    for h in heads:
        blocks = [jnp.dot(scores[h, b], v[h][block(b)], preferred_element_type=F32)
                  for b in range(n_blocks)]
        for n, (rows, cols, _, _) in enumerate(off_diag):
            o_rc = _dot(raw[h, "off", n], v[h][cols])
            for b in range(rows.start // diag, rows.stop // diag):
                lo = b * diag - rows.start
                blocks[b] = blocks[b] + o_rc[lo:lo + diag]
        o = blocks[0] if n_blocks == 1 else jnp.concatenate(blocks, axis=0)
        outs.append(o + inter[h])
    yield
    for h in heads:
        o = outs[h]
        g_h = z_ref[:, 1024 + h * LANES:1024 + (h + 1) * LANES]
        y = o * lax.rsqrt(jnp.mean(o * o, axis=-1, keepdims=True) + EPS)
        y = y * gn_ref[:, h * LANES:(h + 1) * LANES]
        o_ref[:, h * LANES:(h + 1) * LANES] = (y * (g_h * _sigmoid(g_h))).astype(o_ref.dtype)
        st_ref[h] = old_state[h] * s_dec[:, pair(h)] + state_upd[h]


def _ret_constants(tc):
    log_gamma = np.log1p(-np.exp2(-5.0 - np.arange(C_HEADS, dtype=np.float64)))
    idx = np.arange(tc)
    diff = idx[:, None] - idx[None, :]
    dmat = np.where(diff >= 0, np.exp(log_gamma[:, None, None] * np.maximum(diff, 0)[None]), 0.0)
    per_lane = np.repeat(log_gamma, HD)
    qdec = np.exp((idx[:, None] + 1) * per_lane[None, :])
    kdec = np.exp((tc - 1 - idx[:, None]) * per_lane[None, :])
    sdec = np.exp(tc * per_lane)[None, :]
    return [jnp.asarray(a, F32) for a in (dmat, qdec, kdec, sdec)]


def _gla_levels(tc):
    levels = []
    size = tc
    while size >= 2 * SUB:
        levels.append(size)
        size //= 2
    return tuple(levels)


def _gla_diag(tc):
    return min(tc, LANES)


def _gla_constants(tc):
    diag = _gla_diag(tc)
    idx = np.arange(tc)
    tril = (idx[None, :] <= idx[:, None]).astype(np.float32)
    idx = np.arange(diag)
    i, j = idx[:, None], idx[None, :]
    lmasks = []
    for size in (s for s in _gla_levels(tc) if s <= diag):
        lmasks.append(((i // size == j // size) & (i % size >= size // 2) & (j % size < size // 2)))
    lmask = np.stack(lmasks).astype(np.float32)
    bd = (i // SUB == j // SUB).astype(np.float32)
    ind = np.zeros((SUB, 256, LANES), np.float32)
    for jj in range(SUB):
        for h in range(C_HEADS):
            ind[jj, h * HD:(h + 1) * HD, h * SUB + jj] = 1.0
    rep = np.zeros((C_HEADS, LANES, diag), np.float32)
    for h in range(C_HEADS):
        for jj in range(SUB):
            rep[h, h * SUB + jj, jj::SUB] = 1.0
    return (jnp.asarray(tril, BF16), jnp.asarray(lmask, F32), jnp.asarray(bd, F32),
            jnp.asarray(ind, BF16), jnp.asarray(rep, BF16))


def _state_spec():
    return pl.BlockSpec((None, C_HEADS, LANES, LANES), lambda b, i: (b, 0, 0, 0))


def _recurrent_operands(z, consts, l, gn, s0, tc, n_tiles):
    in_specs = [pl.BlockSpec((tc, z.shape[1]), lambda b, i: (b * n_tiles + i, 0))]
    in_specs += [_const_spec(c.shape, (0,) * c.ndim) for c in consts]
    in_specs += [_const_spec((None, 1, 512), (l, 0, 0)), _state_spec()]
    return in_specs, [z, *consts, gn, s0]


def _mixers_kernel(*refs, bodies, n_in):
    ends = np.cumsum(n_in)
    a_in, b_in, c_in, d_in = (refs[e - n:e] for e, n in zip(ends, n_in))
    o_ref, sc_ref, sd_ref, stc_ref, std_ref = refs[ends[-1]:]
    *c_in, s0c_ref = c_in
    *d_in, s0d_ref = d_in
    i = pl.program_id(1)

    @pl.when(i == 0)
    def _():
        stc_ref[...] = s0c_ref[...]
        std_ref[...] = s0d_ref[...]

    body_a, body_b, body_c, body_d = bodies
    running = [body_d(*d_in, o_ref.at[:, 1536:2048], std_ref),
               body_b(*b_in, o_ref.at[:, 512:1024]),
               body_a(*a_in, o_ref.at[:, 0:512]),
               body_c(*c_in, o_ref.at[:, 1024:1536], stc_ref)]
    while running:
        for gen in list(running):
            if next(gen, "done") == "done":
                running.remove(gen)

    @pl.when(i == pl.num_programs(1) - 1)
    def _():
        sc_ref[...] = stc_ref[...]
        sd_ref[...] = std_ref[...]


def _mixers_call(parts, bodies, grid, rows, row_index, vmem_mib, name):
    in_specs = [s for specs, _ in parts for s in specs]
    args = [a for _, arrs in parts for a in arrs]
    n_in = tuple(len(specs) for specs, _ in parts)
    n_rows = grid[0] * grid[1] * rows
    state_shape = jax.ShapeDtypeStruct((grid[0], C_HEADS, LANES, LANES), F32)
    return pl.pallas_call(
        functools.partial(_mixers_kernel, bodies=bodies, n_in=n_in),
        grid=grid,
        in_specs=in_specs,
        out_specs=[pl.BlockSpec((rows, 2048), lambda b, i: (row_index(b, i), 0)),
                   _state_spec(), _state_spec()],
        out_shape=[jax.ShapeDtypeStruct((n_rows, 2048), BF16), state_shape, state_shape],
        scratch_shapes=[pltpu.VMEM((C_HEADS, LANES, LANES), F32)] * 2,
        compiler_params=_params(2, vmem_mib),
        name=name,
    )(*args)


def _state_to_kernel(s):
    st = jnp.swapaxes(s, -1, -2)
    lo = jnp.pad(st, ((0, 0), (0, 0), (0, 0), (0, HD)))
    hi = jnp.pad(st, ((0, 0), (0, 0), (0, 0), (HD, 0)))
    odd = (jnp.arange(C_HEADS) % 2 == 1)[None, :, None, None]
    return jnp.where(odd, hi, lo)


def _state_from_kernel(st):
    lo = st[..., :HD]
    hi = st[..., HD:]
    odd = (jnp.arange(C_HEADS) % 2 == 1)[None, :, None, None]
    return jnp.swapaxes(jnp.where(odd, hi, lo), -1, -2)


def _merge_kernel(x_ref, g_ref, o_ref, wm_ref, wbr_ref, wo_ref, y_ref):
    x = x_ref[...]
    d = x.shape[1]
    h = _rms_rows(x, g_ref[...]).astype(BF16)
    acc = None
    for b in range(4):
        gate = _sigmoid(jnp.dot(h, wm_ref[:, b * d:(b + 1) * d], preferred_element_type=F32))
        term = gate * jnp.dot(o_ref[:, b * 512:(b + 1) * 512], wbr_ref[b],
                              preferred_element_type=F32)
        acc = term if acc is None else acc + term
    y_ref[...] = x + _dot(acc, wo_ref[...])


def _merge(x, outs, l, W, tm):
    n, d = x.shape
    row = lambda i: (i, 0)
    lay = (l, 0, 0)
    in_specs = [pl.BlockSpec((tm, d), row), _const_spec((None, 1, d), lay),
                pl.BlockSpec((tm, 2048), row),
                _const_spec((None, d, 4 * d), lay),
                _const_spec((None, 4, 512, d), (l, 0, 0, 0)),
                _const_spec((None, d, d), lay)]
    return pl.pallas_call(
        _merge_kernel,
        grid=(n // tm,),
        in_specs=in_specs,
        out_specs=pl.BlockSpec((tm, d), row),
        out_shape=jax.ShapeDtypeStruct((n, d), F32),
        compiler_params=_params(1, 48),
        name="merge",
    )(x, W["norm_mix"], outs, W["w_merge"], W["w_branch"], W["w_out"])


def _cross_ffn_kernel(x_ref, gx_ref, wq_ref, qn_ref, mk_ref, mv_ref, wxo_ref, gf_ref, wup_ref,
                      wdn_ref, y_ref, *, n_seq):
    x = x_ref[...]
    d = x.shape[1]
    rows = x.shape[0] // n_seq
    mlen = mk_ref.shape[0] // n_seq
    hx = _rms_rows(x, gx_ref[...])
    q = _dot(hx, wq_ref[...])
    slab = lambda h: slice(h * LANES, (h + 1) * LANES)
    units = [(h, b) for h in range(4) for b in range(n_seq)]
    mem_of = lambda b: slice(b * mlen, (b + 1) * mlen)
    q_n = [(_rms_rows(q[:, slab(h)], qn_ref[...]) * (LANES ** -0.5 * LOG2E)).astype(BF16)
           for h in range(4)]
    scores = [_dot_nt(q_n[h][b * rows:(b + 1) * rows], mk_ref[mem_of(b), slab(h)])
              for h, b in units]
    probs = []
    for s in scores:
        p = jnp.exp2(s - jnp.max(s, axis=-1, keepdims=True))
        probs.append((p.astype(BF16), jnp.sum(p, axis=-1, keepdims=True)))
    outs = [_dot(p, mv_ref[mem_of(b), slab(h)]) / den for (h, b), (p, den) in zip(units, probs)]
    heads = []
    for h in range(4):
        per_seq = outs[h * n_seq:(h + 1) * n_seq]
        heads.append(per_seq[0] if n_seq == 1 else jnp.concatenate(per_seq, axis=0))
    x = x + _dot(jnp.concatenate(heads, axis=1), wxo_ref[...])
    hf = _rms_rows(x, gf_ref[...]).astype(BF16)
    acc = None
    for j in range(wup_ref.shape[1] // d):
        sl = slice(j * d, (j + 1) * d)
        u = jnp.maximum(jnp.dot(hf, wup_ref[:, sl], preferred_element_type=F32), 0.0)
        term = _dot(u * u, wdn_ref[sl, :])
        acc = term if acc is None else acc + term
    y_ref[...] = x + acc


def _cross_ffn(x, mk, mv, l, W, grid, rows, x_index, n_seq, mem_rows, mem_index):
    n, d = x.shape
    lay = (l, 0, 0)
    dff = W["w_up"].shape[-1]
    in_specs = [pl.BlockSpec((rows, d), lambda b, i: (x_index(b, i), 0)),
                _const_spec((None, 1, d), lay),
                _const_spec((None, d, 512), lay),
                _const_spec((None, 1, LANES), lay),
                pl.BlockSpec((mem_rows, 512), lambda b, i: (mem_index(b, i), 0)),
                pl.BlockSpec((mem_rows, 512), lambda b, i: (mem_index(b, i), 0)),
                _const_spec((None, 512, d), lay),
                _const_spec((None, 1, d), lay),
                _const_spec((None, d, dff), lay),
                _const_spec((None, dff, d), lay)]
    return pl.pallas_call(
        functools.partial(_cross_ffn_kernel, n_seq=n_seq),
        grid=grid,
        in_specs=in_specs,
        out_specs=pl.BlockSpec((rows, d), lambda b, i: (x_index(b, i), 0)),
        out_shape=jax.ShapeDtypeStruct((n, d), F32),
        compiler_params=_params(2, 48),
        name="cross_ffn",
    )(x, W["norm_x"], W["w_xq"], W["qn_x"], mk, mv, W["w_xo"], W["norm_ffn"], W["w_up"],
      W["w_down"])


def _memkv_kernel(mem_ref, g_ref, w_ref, kn_ref, k_ref, v_ref):
    hm = _rms_rows(mem_ref[...], g_ref[...])
    kv = _dot(hm, w_ref[...])
    for h in range(4):
        sl = slice(h * LANES, (h + 1) * LANES)
        k_ref[:, sl] = _rms_rows(kv[:, sl], kn_ref[...])
    v_ref[...] = kv[:, 512:1024]


def _memkv(mem, W, depth):
    n, d = mem.shape
    mlen = 256
    batch = n // mlen
    lay = lambda l, b: (l, 0, 0)
    return pl.pallas_call(
        _memkv_kernel,
        grid=(depth, batch),
        in_specs=[pl.BlockSpec((mlen, d), lambda l, b: (b, 0)),
                  pl.BlockSpec((None, 1, d), lay),
                  pl.BlockSpec((None, d, 1024), lay),
                  pl.BlockSpec((None, 1, LANES), lay)],
        out_specs=[pl.BlockSpec((None, mlen, 512), lambda l, b: (l, b, 0))] * 2,
        out_shape=[jax.ShapeDtypeStruct((depth, n, 512), F32)] * 2,
        compiler_params=_params(2, 32),
        name="memkv",
    )(mem, W["norm_mem"], W["w_xkv"], W["kn_x"])


def _front_kernel(*refs, n_in, bodies, prev_a, prev_b):
    ends = np.cumsum(n_in)
    proj_in, a_in, b_in, c_in, d_in = (refs[e - n:e] for e, n in zip(ends, n_in))
    (o_ref, kva_ref, kvb_ref, sc_ref, sd_ref,
     qa_s, hist_a, qb_s, hist_b, zc_s, zd_s, stc_ref, std_ref) = refs[ends[-1]:]
    *c_in, s0c_ref = c_in
    *d_in, s0d_ref = d_in
    tq = qa_s.shape[0]
    i = pl.program_id(1)

    @pl.when(i == 0)
    def _():
        hist_a[0:prev_a, :] = jnp.zeros((prev_a, hist_a.shape[1]), F32)
        hist_b[0:prev_b, :] = jnp.zeros((prev_b, hist_b.shape[1]), F32)
        stc_ref[...] = s0c_ref[...]
        std_ref[...] = s0d_ref[...]

    cur_a = slice(prev_a, prev_a + tq)
    cur_b = slice(prev_b, prev_b + tq)
    dst = dict(a_q=qa_s, a_k=hist_a.at[cur_a, 0:256], a_v=hist_a.at[cur_a, 256:512],
               b_q=qb_s, b_k=hist_b.at[cur_b, 0:512], b_v=hist_b.at[cur_b, 512:1024],
               c=zc_s, d=zd_s)
    body_a, body_b, body_c, body_d = bodies
    (sink_ref,), (bias_ref,) = a_in, b_in
    n_prev_b = prev_b // tq
    k_b = [hist_b.at[j * tq:(j + 1) * tq, 0:512] for j in range(n_prev_b + 1)]
    v_b = [hist_b.at[j * tq:(j + 1) * tq, 512:1024] for j in range(n_prev_b + 1)]
    starts = {
        2: lambda: body_a(qa_s, hist_a.at[0:prev_a, 0:256], hist_a.at[cur_a, 0:256],
                          hist_a.at[0:prev_a, 256:512], hist_a.at[cur_a, 256:512], sink_ref,
                          o_ref.at[:, 0:512]),
        3: lambda: body_b(qb_s, *k_b, *v_b, bias_ref, o_ref.at[:, 512:1024]),
        4: lambda: body_c(zc_s, *c_in, o_ref.at[:, 1024:1536], stc_ref),
        5: lambda: body_d(zd_s, *d_in, o_ref.at[:, 1536:2048], std_ref),
    }
    proj = _inproj_stages(*proj_in, dst)
    running = [proj]
    proj_stage = 0
    while running:
        for gen in list(running):
            if next(gen, "done") == "done":
                running.remove(gen)
            if gen is proj:
                proj_stage += 1
                if proj_stage in starts:
                    running.append(starts[proj_stage]())

    low = _lane_iota((1, LANES)) < HD
    for t, col0 in enumerate((0, 256)):
        kva_ref[:, t * LANES:(t + 1) * LANES] = jnp.where(
            low, hist_a[cur_a, col0:col0 + LANES], hist_a[cur_a, col0 + LANES:col0 + 2 * LANES])
    kvb_ref[...] = hist_b[cur_b, :]
    hist_a[0:prev_a, :] = hist_a[tq:tq + prev_a, :]
    hist_b[0:prev_b, :] = hist_b[tq:tq + prev_b, :]

    @pl.when(i == pl.num_programs(1) - 1)
    def _():
        sc_ref[...] = stc_ref[...]
        sd_ref[...] = std_ref[...]


def _front(x, l, W, cos, sin, relbias, batch, seq):
    n, d = x.shape
    tq = SEQ_TILE
    n_tiles = seq // tq
    cur = lambda b, i: b * n_tiles + i
    sink_rows = jnp.broadcast_to(jnp.repeat(W["sink"][l].reshape(2, 1, 4), CHUNK, axis=2),
                                 (2, SUB, 4 * CHUNK))
    bias_b = _rel_bias(relbias, tq, B_PREV) * LOG2E + jnp.asarray(
        np.where(_band_mask(tq, B_PREV, 8), 0.0, NEG)[None], F32)
    zero_state = jnp.zeros((batch, C_HEADS, LANES, LANES), F32)
    whole = lambda a: _const_spec(a.shape, (0,) * a.ndim)
    ret_consts = _ret_constants(tq)
    gla_consts = _gla_constants(tq)
    gain = _const_spec((None, 1, 512), (l, 0, 0))
    parts = [
        (_inproj_specs(l, d, tq, n_tiles, cur), _inproj_args(x, W, cos, sin)),
        ([whole(sink_rows)], [sink_rows]),
        ([whole(bias_b)], [bias_b]),
        ([whole(c) for c in ret_consts] + [gain, _state_spec()],
         [*ret_consts, W["gn_c"], zero_state]),
        ([whole(c) for c in gla_consts] + [gain, _state_spec()],
         [*gla_consts, W["gn_d"], zero_state]),
    ]
    bodies = (functools.partial(_attn_a_cols_kernel, masked_prefix=(A_PREV, tq)),
              functools.partial(_attn_b_kernel, n_kv=B_PREV // tq + 1, masked_prefix=(B_PREV, tq)),
              _ret_kernel,
              functools.partial(_gla_kernel, levels=_gla_levels(tq)))
    row_spec = lambda w: pl.BlockSpec((tq, w), lambda b, i: (cur(b, i), 0))
    state_shape = jax.ShapeDtypeStruct((batch, C_HEADS, LANES, LANES), F32)
    return pl.pallas_call(
        functools.partial(_front_kernel, n_in=tuple(len(p[0]) for p in parts), bodies=bodies,
                          prev_a=A_PREV, prev_b=B_PREV),
        grid=(batch, n_tiles),
        in_specs=[s for p in parts for s in p[0]],
        out_specs=[row_spec(2048), row_spec(256), row_spec(1024), _state_spec(), _state_spec()],
        out_shape=[jax.ShapeDtypeStruct((n, 2048), BF16), jax.ShapeDtypeStruct((n, 256), F32),
                   jax.ShapeDtypeStruct((n, 1024), F32), state_shape, state_shape],
        scratch_shapes=[pltpu.VMEM((tq, 512), F32), pltpu.VMEM((A_PREV + tq, 512), F32),
                        pltpu.VMEM((tq, 512), F32), pltpu.VMEM((B_PREV + tq, 1024), F32),
                        pltpu.VMEM((tq, 1536), F32), pltpu.VMEM((tq, 1792), F32),
                        pltpu.VMEM((C_HEADS, LANES, LANES), F32),
                        pltpu.VMEM((C_HEADS, LANES, LANES), F32)],
        compiler_params=_params(2, 60),
        name="front",
    )(*[a for p in parts for a in p[1]])


def _rope_tables(pos):
    half = HD // 2
    freqs = ROPE_THETA ** (-jnp.arange(half, dtype=F32) / half)
    ang = pos.astype(F32)[:, None] * freqs[None, :]
    cos = jnp.tile(jnp.cos(ang), (1, LANES // half))
    sin = jnp.sin(ang)
    sin = jnp.tile(jnp.concatenate([-sin, sin], axis=1), (1, LANES // HD))
    return cos, sin


def _dup_heads(w, col0, n_heads):
    lead = w.shape[:-1]
    blk = w[..., col0:col0 + n_heads * HD].reshape(*lead, n_heads, 1, HD)
    return jnp.broadcast_to(blk, (*lead, n_heads, 2, HD)).reshape(*lead, n_heads * 2 * HD)


def _prepare_weights(P):
    depth, d, _ = P["w_in"].shape
    w_in = P["w_in"]
    vec = lambda a: a.reshape(depth, 1, -1).astype(F32)
    W = {}
    W["wa"] = jnp.concatenate([w_in[..., 0:512], _dup_heads(w_in, 512, 2), _dup_heads(w_in, 640, 2)],
                              axis=-1).astype(BF16)
    W["w_in"] = w_in.astype(BF16)
    W["wad"] = jnp.pad(w_in[..., 5376:5392], ((0, 0), (0, 0), (0, LANES - GLA_RANK))).astype(BF16)
    W["wup"] = jnp.pad(P["w_alpha_up"], ((0, 0), (0, LANES - GLA_RANK), (0, 0))).astype(BF16)
    W["b_alpha"] = vec(P["b_alpha"])
    W["norm_mix"] = vec(P["norm_mix"])
    W["gqa"] = vec(jnp.tile(P["qn_a"], (1, 8)) * (HD ** -0.5 * LOG2E))
    W["gka"] = vec(jnp.tile(P["kn_a"], (1, 4)))
    W["gqb"] = vec(jnp.tile(P["qn_b"], (1, 8)) * (HD ** -0.5 * LOG2E))
    W["gkb"] = vec(jnp.tile(P["kn_b"], (1, 8)))
    W["gn_c"] = vec(P["gn_c"])
    W["gn_d"] = vec(P["gn_d"])
    W["sink"] = P["sink_a"].astype(F32) * LOG2E
    for name in ("w_merge", "w_branch", "w_out", "w_xq", "w_xkv", "w_xo", "w_up", "w_down"):
        W[name] = P[name].astype(BF16)
    for name in ("norm_x", "norm_mem", "qn_x", "kn_x", "norm_ffn"):
        W[name] = vec(P[name])
    return W


def _band_mask(q_rows, prev_rows, n_prev_chunks):
    r = np.arange(q_rows)[:, None] // CHUNK
    u = np.arange(prev_rows + q_rows)[None, :] // CHUNK - prev_rows // CHUNK
    return (u <= r) & (u >= r - n_prev_chunks)


def _rel_bias(table, q_rows, prev_rows):
    nk = prev_rows + q_rows
    period = q_rows + nk
    j = np.arange(period)
    offset = np.where(j < nk, j, j - period)
    idx = np.clip(prev_rows - offset, -REL_CLIP, REL_CLIP) + REL_CLIP
    v = table[:, idx]
    skew = jnp.tile(v, (1, q_rows))[:, :q_rows * (period - 1)]
    return skew.reshape(table.shape[0], q_rows, period - 1)[:, :, :nk]


def _mixers_sample(slabs, l, W, relbias, cache, batch, seq):
    za, zb, zc, zd = slabs
    win_k, win_v, band_k, band_v, s_c0, s_d0 = cache
    cur = lambda b, i: b
    la = win_k.shape[1]
    lb = band_k.shape[2]
    ck = jnp.broadcast_to(win_k[:, :, :, None, :], (batch, la, 2, 2, HD)).reshape(batch * la, 256)
    cv = jnp.broadcast_to(win_v[:, :, :, None, :], (batch, la, 2, 2, HD)).reshape(batch * la, 256)
    kv_a = [((ck, cv), la, 0, 0, 2, cur), ((za, za), seq, 512, 768, 2, cur)]
    part_a = _attention_operands(za, kv_a, _stacked_sinks(W["sink"][l], seq), seq, cur)
    body_a = functools.partial(_attn_a_kernel, n_kv=len(kv_a), chunk_rows=seq,
                               keys_per_chunk=la + seq, masked_prefix=None)

    band = (band_k.reshape(-1, 512), band_v.reshape(-1, 512))
    kv_b = [(band, lb, 0, 0, 4, lambda b, i: l * batch + b), ((zb, zb), seq, 512, 1024, 4, cur)]
    part_b = _attention_operands(zb, kv_b, _rel_bias(relbias, seq, lb) * LOG2E, seq, cur)
    body_b = functools.partial(_attn_b_kernel, n_kv=len(kv_b), masked_prefix=None)

    part_c = _recurrent_operands(zc, _ret_constants(seq), l, W["gn_c"], _state_to_kernel(s_c0),
                                 seq, 1)
    part_d = _recurrent_operands(zd, _gla_constants(seq), l, W["gn_d"], _state_to_kernel(s_d0),
                                 seq, 1)
    body_d = functools.partial(_gla_kernel, levels=_gla_levels(seq))
    return _mixers_call((part_a, part_b, part_c, part_d), (body_a, body_b, _ret_kernel, body_d),
                        (batch, 1), seq, cur, 40, "mixers_step")


def _tail_rows(kv_a, kv_b, batch, seq, rows_a, rows_b):
    ta = kv_a.reshape(batch, seq, -1)[:, seq - rows_a:]
    tb = kv_b.reshape(batch, seq, -1)[:, seq - rows_b:]
    ka = ta[:, :, 0:128].reshape(batch, rows_a, 2, HD)
    va = ta[:, :, 128:256].reshape(batch, rows_a, 2, HD)
    kb = tb[:, :, 0:512].reshape(batch, rows_b, 8, HD)
    vb = tb[:, :, 512:1024].reshape(batch, rows_b, 8, HD)
    return ka, va, kb, vb


def _slab_kv(slabs):
    za, zb = slabs[0], slabs[1]
    pick = lambda col0: jnp.concatenate([za[:, col0:col0 + HD], za[:, col0 + LANES:col0 + LANES + HD]], 1)
    return jnp.concatenate([pick(512), pick(768)], axis=1), zb[:, 512:1536]


def kernel(x_prompt, x_sample, cache_win_k, cache_win_v, cache_band_k, cache_band_v, state_ret, state_gla, cache_mem_k, cache_mem_v, mem_prompt, norm_mix, w_in, qn_a, kn_a, sink_a, qn_b, kn_b, relbias_b, gn_c, w_alpha_up, b_alpha, gn_d, w_branch, w_merge, w_out, norm_x, norm_mem, w_xq, w_xkv, qn_x, kn_x, w_xo, norm_ffn, w_up, w_down):
    P = dict(norm_mix=norm_mix, w_in=w_in, qn_a=qn_a, kn_a=kn_a, sink_a=sink_a, qn_b=qn_b,
             kn_b=kn_b, relbias_b=relbias_b, gn_c=gn_c, w_alpha_up=w_alpha_up, b_alpha=b_alpha,
             gn_d=gn_d, w_branch=w_branch, w_merge=w_merge, w_out=w_out, norm_x=norm_x,
             norm_mem=norm_mem, w_xq=w_xq, w_xkv=w_xkv, qn_x=qn_x, kn_x=kn_x, w_xo=w_xo,
             norm_ffn=norm_ffn, w_up=w_up, w_down=w_down)
    depth = w_in.shape[0]
    bp, tp, d = x_prompt.shape
    bs, ts, _ = x_sample.shape
    mlen = mem_prompt.shape[1]
    W = _prepare_weights(P)

    cos_p, sin_p = _rope_tables(jnp.arange(tp))
    cos_s, sin_s = _rope_tables(jnp.tile(PAST_LEN + jnp.arange(ts), bs))
    mem_k, mem_v = _memkv(mem_prompt.reshape(bp * mlen, d), W, depth)
    mem_k2 = mem_k.reshape(depth * bp * mlen, 512)
    mem_v2 = mem_v.reshape(depth * bp * mlen, 512)

    xp = x_prompt.reshape(bp * tp, d)
    xs = x_sample.reshape(bs * ts, d)
    tw = min(TOKEN_TILE, tp)
    n_wide = tp // tw
    rows_p, rows_s = [], []
    st_p, st_s = [], []
    for l in range(depth):
        relbias = relbias_b[l].astype(F32)
        outs, kv_a, kv_b, *states = _front(xp, l, W, cos_p, sin_p, relbias, bp, tp)
        rows_p.append(_tail_rows(kv_a, kv_b, bp, tp, min(A_PREV, tp), min(B_PREV, tp)))
        st_p.append(states)
        xp = _merge(xp, outs, l, W, tw)
        xp = _cross_ffn(xp, mem_k2, mem_v2, l, W, (bp, n_wide), tw,
                        lambda b, i: b * n_wide + i, 1, mlen, lambda b, i: l * bp + b)
        slabs = _inproj(xs, l, W, cos_s, sin_s, 1, bs * ts)
        cache = (cache_win_k[l], cache_win_v[l], cache_band_k, cache_band_v,
                 state_ret[l], state_gla[l])
        outs, *states = _mixers_sample(slabs, l, W, relbias, cache, bs, ts)
        rows_s.append(_tail_rows(*_slab_kv(slabs), bs, ts, ts, ts))
        st_s.append(states)
        xs = _merge(xs, outs, l, W, bs * ts)
        xs = _cross_ffn(xs, cache_mem_k.reshape(-1, 512), cache_mem_v.reshape(-1, 512), l, W,
                        (1, 1), bs * ts, lambda b, i: 0, bs, bs * mlen, lambda b, i: l)

    def stack_rows(rows, k):
        return jnp.stack([r[k] for r in rows])

    def stack_state(states, k):
        return jnp.stack([_state_from_kernel(s[k]) for s in states])

    return (xp.reshape(bp, tp, d), xs.reshape(bs, ts, d),
            stack_rows(rows_p, 0), stack_rows(rows_p, 1), stack_rows(rows_p, 2), stack_rows(rows_p, 3),
            stack_state(st_p, 0), stack_state(st_p, 1),
            mem_k.reshape(depth, bp, mlen, 4, LANES), mem_v.reshape(depth, bp, mlen, 4, LANES),
            stack_rows(rows_s, 0), stack_rows(rows_s, 1), stack_rows(rows_s, 2), stack_rows(rows_s, 3),
            stack_state(st_s, 0), stack_state(st_s, 1))
```
